```python
import jax, jax.numpy as jnp
from jax import lax
import numpy as np

D_MODEL = 1024
BATCH = 2
SEQ = 8192
DEPTH = 4

N_MIXERS = 2
CONV_WIDTH = 3
CONV_GROUPS = 16
FOURIER_GROUPS = 8
FOURIER_GROUP_DIM = D_MODEL // FOURIER_GROUPS
D_FF = ((8 * D_MODEL // 3 + 255) // 256) * 256
N_CONV_LAYERS = (DEPTH + 1) // 2
N_FOURIER_LAYERS = DEPTH // 2
RMS_EPS = 1e-5

kernel_name = "hybrid_shortconv_fourier_encoder"


def rmsnorm(x, g):
    xf = x.astype(jnp.float32)
    r = lax.rsqrt(jnp.mean(xf * xf, axis=-1, keepdims=True) + RMS_EPS)
    return (xf * r * g.astype(jnp.float32)).astype(x.dtype)


def short_conv_mixer(h, w_in, k, w_out):
    bcv = jnp.einsum('bsd,de->bse', h, w_in)
    gate_b, gate_c, v = jnp.split(bcv, 3, axis=-1)
    u = gate_c * v
    up = jnp.pad(u, ((0, 0), (1, 1), (0, 0)))
    conv = k[0] * up[:, :-2] + k[1] * up[:, 1:-1] + k[2] * up[:, 2:]
    return jnp.einsum('bsd,de->bse', gate_b * conv, w_out)


def fourier_mixer(h, w_out):
    bsz, seq, dm = h.shape
    hf = h.astype(jnp.float32).reshape(bsz, seq, FOURIER_GROUPS, FOURIER_GROUP_DIM)
    f = jnp.fft.fft2(hf, axes=(1, 3), norm='ortho')
    y = jnp.real(f).astype(h.dtype).reshape(bsz, seq, dm)
    return jnp.einsum('bsd,de->bse', y, w_out)


def swiglu(h, w_gate, w_up, w_down):
    a = jnp.einsum('bsd,df->bsf', h, w_gate)
    b = jnp.einsum('bsd,df->bsf', h, w_up)
    return jnp.einsum('bsf,fd->bsd', jax.nn.silu(a) * b, w_down)


def setup_inputs(seed: int = 0) -> dict:
    key = jax.random.key(seed)
    ks = jax.random.split(key, 12)
    f32 = jnp.float32
    x = jax.random.normal(ks[0], (BATCH, SEQ, D_MODEL), f32)
    conv_w_in = jax.random.normal(ks[1], (N_CONV_LAYERS, D_MODEL, 3 * D_MODEL), f32) * D_MODEL ** -0.5
    conv_k = jax.random.normal(ks[2], (N_CONV_LAYERS, CONV_WIDTH, D_MODEL), f32) * CONV_WIDTH ** -0.5
    conv_w_out = jax.random.normal(ks[3], (N_CONV_LAYERS, D_MODEL, D_MODEL), f32) * D_MODEL ** -0.5
    fourier_w_out = jax.random.normal(ks[4], (N_FOURIER_LAYERS, D_MODEL, D_MODEL), f32) * D_MODEL ** -0.5
    mix_norm_g = 1.0 + 0.02 * jax.random.normal(ks[5], (DEPTH, D_MODEL), f32)
    ffn_norm_g = 1.0 + 0.02 * jax.random.normal(ks[6], (DEPTH, D_MODEL), f32)
    ffn_w_gate = jax.random.normal(ks[7], (DEPTH, D_MODEL, D_FF), f32) * D_MODEL ** -0.5
    ffn_w_up = jax.random.normal(ks[8], (DEPTH, D_MODEL, D_FF), f32) * D_MODEL ** -0.5
    ffn_w_down = jax.random.normal(ks[9], (DEPTH, D_FF, D_MODEL), f32) * D_FF ** -0.5
    final_norm_g = 1.0 + 0.02 * jax.random.normal(ks[10], (D_MODEL,), f32)
    return {
        'x': x,
        'conv_w_in': conv_w_in,
        'conv_k': conv_k,
        'conv_w_out': conv_w_out,
        'fourier_w_out': fourier_w_out,
        'mix_norm_g': mix_norm_g,
        'ffn_norm_g': ffn_norm_g,
        'ffn_w_gate': ffn_w_gate,
        'ffn_w_up': ffn_w_up,
        'ffn_w_down': ffn_w_down,
        'final_norm_g': final_norm_g,
    }


def reference(x, conv_w_in, conv_k, conv_w_out, fourier_w_out, mix_norm_g,
              ffn_norm_g, ffn_w_gate, ffn_w_up, ffn_w_down, final_norm_g):
    h = x
    for i in range(DEPTH):
        hn = rmsnorm(h, mix_norm_g[i])
        j = i // N_MIXERS
        if i % N_MIXERS == 0:
            h = h + short_conv_mixer(hn, conv_w_in[j], conv_k[j], conv_w_out[j])
        else:
            h = h + fourier_mixer(hn, fourier_w_out[j])
        hn = rmsnorm(h, ffn_norm_g[i])
        h = h + swiglu(hn, ffn_w_gate[i], ffn_w_up[i], ffn_w_down[i])
    return rmsnorm(h, final_norm_g)
```

```python
import functools

import numpy as np
import jax
import jax.numpy as jnp
from jax import lax
from jax.experimental import pallas as pl
from jax.experimental.pallas import tpu as pltpu

RMS_EPS = 1e-5
FOURIER_GROUPS = 8

V7X_LANES = 128
V7X_BF16_SUBLANES = 16
V7X_VMEM_LIMIT_BYTES = 56 * 1024 * 1024

_BF16 = jnp.bfloat16
_F32 = jnp.float32


def _dot(a, b):
    return jnp.dot(a, b, preferred_element_type=_F32)


def _rmsnorm(x, g):
    r = lax.rsqrt(jnp.mean(x * x, axis=-1, keepdims=True) + RMS_EPS)
    return x * r * g


def _resident(shape):
    return pl.BlockSpec(shape, lambda *_: (0,) * len(shape), pipeline_mode=pl.Buffered(1))


def _conv_mixer_kernel(h_ref, hprev_ref, hnext_ref, g_ref, win_ref, k_ref, m_ref,
                       xn_ref, u_ref, *, tm, halo, d):
    i = pl.program_id(1)
    g = g_ref[...]
    xn_ref[0:halo] = _rmsnorm(hprev_ref[...], g).astype(_BF16)
    xn_ref[halo:halo + tm] = _rmsnorm(h_ref[...], g).astype(_BF16)
    xn_ref[halo + tm:] = _rmsnorm(hnext_ref[...], g).astype(_BF16)

    c = _dot(xn_ref[...], win_ref[:, d:2 * d])
    v = _dot(xn_ref[...], win_ref[:, 2 * d:3 * d])
    u_ref[...] = c * v

    @pl.when(i == 0)
    def _():
        u_ref[halo - 8:halo] = jnp.zeros((8, d), _F32)

    @pl.when(i == pl.num_programs(1) - 1)
    def _():
        u_ref[halo + tm:halo + tm + 8] = jnp.zeros((8, d), _F32)

    conv = (k_ref[0:1] * u_ref[halo - 1:halo - 1 + tm]
            + k_ref[1:2] * u_ref[halo:halo + tm]
            + k_ref[2:3] * u_ref[halo + 1:halo + 1 + tm])
    b = _dot(xn_ref[halo:halo + tm], win_ref[:, 0:d])
    m_ref[...] = (b * conv).astype(_BF16)


def _conv_mixer(h, g, w_in, k, *, tm):
    bsz, seq, d = h.shape
    halo = V7X_BF16_SUBLANES
    nt = seq // tm
    hb = tm // halo
    last = seq // halo - 1
    kern = functools.partial(_conv_mixer_kernel, tm=tm, halo=halo, d=d)
    return pl.pallas_call(
        kern,
        out_shape=jax.ShapeDtypeStruct((bsz, seq, d), _BF16),
        grid=(bsz, nt),
        in_specs=[
            pl.BlockSpec((None, tm, d), lambda b, i: (b, i, 0)),
            pl.BlockSpec((None, halo, d), lambda b, i: (b, jnp.maximum(i * hb - 1, 0), 0)),
            pl.BlockSpec((None, halo, d), lambda b, i: (b, jnp.minimum((i + 1) * hb, last), 0)),
            _resident((1, d)),
            _resident((d, 3 * d)),
            _resident((3, d)),
        ],
        out_specs=pl.BlockSpec((None, tm, d), lambda b, i: (b, i, 0)),
        scratch_shapes=[
            pltpu.VMEM((tm + 2 * halo, d), _BF16),
            pltpu.VMEM((tm + 2 * halo, d), _F32),
        ],
        compiler_params=pltpu.CompilerParams(
            dimension_semantics=("parallel", "parallel"),
            vmem_limit_bytes=V7X_VMEM_LIMIT_BYTES),
        name="conv_mixer",
    )(h, h, h, g, w_in, k)


def _dft_tables(seq, gd):
    n2 = V7X_LANES
    n1 = seq // n2
    two_pi = 2.0 * np.pi
    th1 = two_pi * np.outer(np.arange(n1), np.arange(n1)) / n1
    f1 = np.stack([np.cos(th1), -np.sin(th1)], axis=1).reshape(2 * n1, n1) / np.sqrt(n1)
    th2 = two_pi * np.outer(np.arange(n2), np.arange(n2)) / n2
    c2 = np.cos(th2) / np.sqrt(n2)
    s2 = np.sin(th2) / np.sqrt(n2)
    tw = two_pi * np.outer(np.arange(n1), np.arange(n2)) / seq
    ph = two_pi * np.outer(np.arange(gd), np.arange(gd)) / gd
    cs = np.concatenate([np.cos(ph), np.sin(ph)], axis=0) / np.sqrt(gd)
    f = lambda a: jnp.asarray(a, _F32)
    return n1, n2, f(f1), f(c2), f(s2), f(np.cos(tw)), f(np.sin(tw)), f(cs)


def _fourier_stage1_kernel(h_ref, g_ref, f1_ref, t_ref, *, d, chunks):
    g = g_ref[...]
    f1 = f1_ref[...]
    for c in range(chunks):
        xn = _rmsnorm(h_ref[:, c * d:(c + 1) * d], g).astype(_BF16)
        t_ref[:, c * d:(c + 1) * d] = _dot(f1, xn).astype(_BF16)


def _fourier_stage1(h, g, f1, *, n1, n2, chunks):
    bsz, seq, d = h.shape
    ct = chunks * d
    kern = functools.partial(_fourier_stage1_kernel, d=d, chunks=chunks)
    return pl.pallas_call(
        kern,
        out_shape=jax.ShapeDtypeStruct((bsz, 2 * n1, n2 * d), _BF16),
        grid=(bsz, n2 * d // ct),
        in_specs=[
            pl.BlockSpec((None, n1, ct), lambda b, j: (b, 0, j)),
            _resident((1, d)),
            _resident((2 * n1, n1)),
        ],
        out_specs=pl.BlockSpec((None, 2 * n1, ct), lambda b, j: (b, 0, j)),
        compiler_params=pltpu.CompilerParams(
            dimension_semantics=("parallel", "parallel"),
            vmem_limit_bytes=V7X_VMEM_LIMIT_BYTES),
        name="fourier_stage1",
    )(h.reshape(bsz, n1, n2 * d), g, f1)


def _fourier_stage2_kernel(t_ref, c2_ref, s2_ref, twc_ref, tws_ref, cs_ref, y_ref, g_scr,
                           *, kb, n2, d, gd):
    c2 = c2_ref[...]
    s2 = s2_ref[...]
    cs = cs_ref[...]
    for kk in range(kb):
        twc = twc_ref[kk:kk + 1]
        tws = tws_ref[kk:kk + 1]
        cth = c2 * twc - s2 * tws
        sth = s2 * twc + c2 * tws
        g_scr[0:n2, 0:n2] = cth.astype(_BF16)
        g_scr[0:n2, n2:2 * n2] = sth.astype(_BF16)
        g_scr[n2:2 * n2, 0:n2] = (-sth).astype(_BF16)
        g_scr[n2:2 * n2, n2:2 * n2] = cth.astype(_BF16)
        u = _dot(g_scr[...], t_ref[kk * 2 * n2:(kk + 1) * 2 * n2]).astype(_BF16)
        for gi in range(d // gd):
            lhs = jnp.concatenate(
                [u[0:n2, gi * gd:(gi + 1) * gd], u[n2:2 * n2, gi * gd:(gi + 1) * gd]], axis=1)
            y_ref[:, kk * d + gi * gd:kk * d + (gi + 1) * gd] = _dot(lhs, cs).astype(_BF16)


def _fourier_stage2(t1, c2, s2, twc, tws, cs, *, n1, n2, d, kb):
    bsz = t1.shape[0]
    gd = d // FOURIER_GROUPS
    kern = functools.partial(_fourier_stage2_kernel, kb=kb, n2=n2, d=d, gd=gd)
    return pl.pallas_call(
        kern,
        out_shape=jax.ShapeDtypeStruct((bsz, n2, n1 * d), _BF16),
        grid=(bsz, n1 // kb),
        in_specs=[
            pl.BlockSpec((None, kb * 2 * n2, d), lambda b, j: (b, j, 0)),
            _resident((n2, n2)),
            _resident((n2, n2)),
            pl.BlockSpec((kb, n2), lambda b, j: (j, 0)),
            pl.BlockSpec((kb, n2), lambda b, j: (j, 0)),
            _resident((2 * gd, gd)),
        ],
        out_specs=pl.BlockSpec((None, n2, kb * d), lambda b, j: (b, 0, j)),
        scratch_shapes=[pltpu.VMEM((2 * n2, 2 * n2), _BF16)],
        compiler_params=pltpu.CompilerParams(
            dimension_semantics=("parallel", "parallel"),
            vmem_limit_bytes=V7X_VMEM_LIMIT_BYTES),
        name="fourier_stage2",
    )(t1.reshape(bsz, 2 * n1 * n2, d), c2, s2, twc, tws, cs)


def _ffn_kernel(h_ref, m_ref, gn_ref, wo_ref, wg_ref, wu_ref, wd_ref, gf_ref, o_ref,
                hn_scr, act_scr, *, dff, fchunk, final):
    h1 = h_ref[...] + _dot(m_ref[...], wo_ref[...])
    o_ref[...] = h1
    hn_scr[...] = _rmsnorm(h1, gn_ref[...]).astype(_BF16)
    for c0 in range(0, dff, fchunk):
        a = _dot(hn_scr[...], wg_ref[:, c0:c0 + fchunk])
        b = _dot(hn_scr[...], wu_ref[:, c0:c0 + fchunk])
        act_scr[:, c0:c0 + fchunk] = (a * jax.nn.sigmoid(a) * b).astype(_BF16)
    out = o_ref[...] + _dot(act_scr[...], wd_ref[...])
    if final:
        out = _rmsnorm(out, gf_ref[...])
    o_ref[...] = out


def _ffn(h, m, gn, wo, wg, wu, wd, gf, *, tm, fchunk, final):
    n, d = h.shape
    dff = wg.shape[1]
    kern = functools.partial(_ffn_kernel, dff=dff, fchunk=fchunk, final=final)
    row = lambda i: (i, 0)
    return pl.pallas_call(
        kern,
        out_shape=jax.ShapeDtypeStruct((n, d), _F32),
        grid=(n // tm,),
        in_specs=[
            pl.BlockSpec((tm, d), row),
            pl.BlockSpec((tm, d), row),
            _resident((1, d)),
            _resident((d, d)),
            _resident((d, dff)),
            _resident((d, dff)),
            _resident((dff, d)),
            _resident((1, d)),
        ],
        out_specs=pl.BlockSpec((tm, d), row),
        scratch_shapes=[
            pltpu.VMEM((tm, d), _BF16),
            pltpu.VMEM((tm, dff), _BF16),
        ],
        compiler_params=pltpu.CompilerParams(
            dimension_semantics=("parallel",),
            vmem_limit_bytes=V7X_VMEM_LIMIT_BYTES),
        name="outproj_swiglu",
    )(h, m, gn, wo, wg, wu, wd, gf)


def kernel(x, conv_w_in, conv_k, conv_w_out, fourier_w_out, mix_norm_g, ffn_norm_g,
           ffn_w_gate, ffn_w_up, ffn_w_down, final_norm_g):
    bsz, seq, d = x.shape
    depth = mix_norm_g.shape[0]
    dff = ffn_w_gate.shape[-1]
    assert d % FOURIER_GROUPS == 0 and seq % (V7X_LANES * V7X_BF16_SUBLANES) == 0

    tm_conv = 512
    tm_ffn = 512
    fchunk = 256
    assert seq % tm_conv == 0 and (bsz * seq) % tm_ffn == 0 and dff % fchunk == 0

    n1, n2, f1, c2, s2, twc, tws, cs = _dft_tables(seq, d // FOURIER_GROUPS)
    f1, cs = f1.astype(_BF16), cs.astype(_BF16)
    kb = min(8, n1)
    assert n1 % kb == 0

    bf = lambda w: w.astype(_BF16)
    conv_w_in, conv_w_out, fourier_w_out = bf(conv_w_in), bf(conv_w_out), bf(fourier_w_out)
    ffn_w_gate, ffn_w_up, ffn_w_down = bf(ffn_w_gate), bf(ffn_w_up), bf(ffn_w_down)
    gfin = final_norm_g.reshape(1, d)

    h = x
    for i in range(depth):
        j = i // 2
        gmix = mix_norm_g[i].reshape(1, d)
        if i % 2 == 0:
            m = _conv_mixer(h, gmix, conv_w_in[j], conv_k[j], tm=tm_conv)
            wo = conv_w_out[j]
        else:
            t1 = _fourier_stage1(h, gmix, f1, n1=n1, n2=n2, chunks=8)
            m = _fourier_stage2(t1, c2, s2, twc, tws, cs, n1=n1, n2=n2, d=d, kb=kb)
            wo = fourier_w_out[j]
        h = _ffn(h.reshape(bsz * seq, d), m.reshape(bsz * seq, d),
                 ffn_norm_g[i].reshape(1, d), wo, ffn_w_gate[i], ffn_w_up[i], ffn_w_down[i],
                 gfin, tm=tm_ffn, fchunk=fchunk, final=(i == depth - 1)).reshape(bsz, seq, d)
    return h
```

```python
import functools

import numpy as np
import jax
import jax.numpy as jnp
from jax import lax
from jax.experimental import pallas as pl
from jax.experimental.pallas import tpu as pltpu

RMS_EPS = 1e-5
FOURIER_GROUPS = 8

V7X_LANES = 128
V7X_F32_SUBLANES = 8
V7X_BF16_SUBLANES = 16
V7X_MXU_DIM = 256
V7X_VMEM_LIMIT_BYTES = 56 * 1024 * 1024

_BF16 = jnp.bfloat16
_F32 = jnp.float32


def _dot(a, b):
    return jnp.dot(a, b, preferred_element_type=_F32)


def _rmsnorm(x, g):
    r = lax.rsqrt(jnp.mean(x * x, axis=-1, keepdims=True) + RMS_EPS)
    return x * r * g


def _resident(shape):
    return pl.BlockSpec(shape, lambda *_: (0,) * len(shape), pipeline_mode=pl.Buffered(1))


def _params(*semantics):
    return pltpu.CompilerParams(dimension_semantics=semantics,
                                vmem_limit_bytes=V7X_VMEM_LIMIT_BYTES)


def _conv_mixer_kernel(h_ref, hprev_ref, hnext_ref, g_ref, win_ref, k_ref, m_ref,
                       xn_ref, u_ref, *, tm, halo, d):
    i = pl.program_id(1)
    g = g_ref[...]
    xn_ref[0:halo] = _rmsnorm(hprev_ref[...], g).astype(_BF16)
    xn_ref[halo:halo + tm] = _rmsnorm(h_ref[...], g).astype(_BF16)
    xn_ref[halo + tm:] = _rmsnorm(hnext_ref[...], g).astype(_BF16)

    c = _dot(xn_ref[...], win_ref[:, d:2 * d])
    v = _dot(xn_ref[...], win_ref[:, 2 * d:3 * d])
    u_ref[...] = c * v

    zeros = jnp.zeros((V7X_F32_SUBLANES, d), _F32)

    @pl.when(i == 0)
    def _():
        u_ref[halo - V7X_F32_SUBLANES:halo] = zeros

    @pl.when(i == pl.num_programs(1) - 1)
    def _():
        u_ref[halo + tm:halo + tm + V7X_F32_SUBLANES] = zeros

    conv = (k_ref[0:1] * u_ref[halo - 1:halo - 1 + tm]
            + k_ref[1:2] * u_ref[halo:halo + tm]
            + k_ref[2:3] * u_ref[halo + 1:halo + 1 + tm])
    b = _dot(xn_ref[halo:halo + tm], win_ref[:, 0:d])
    m_ref[...] = (b * conv).astype(_BF16)


def _conv_mixer(h, g, w_in, k, *, tm):
    bsz, seq, d = h.shape
    halo = V7X_BF16_SUBLANES
    nt = seq // tm
    hb = tm // halo
    last = seq // halo - 1
    kern = functools.partial(_conv_mixer_kernel, tm=tm, halo=halo, d=d)
    return pl.pallas_call(
        kern,
        out_shape=jax.ShapeDtypeStruct((bsz, seq, d), _BF16),
        grid=(bsz, nt),
        in_specs=[
            pl.BlockSpec((None, tm, d), lambda b, i: (b, i, 0)),
            pl.BlockSpec((None, halo, d), lambda b, i: (b, jnp.maximum(i * hb - 1, 0), 0)),
            pl.BlockSpec((None, halo, d), lambda b, i: (b, jnp.minimum((i + 1) * hb, last), 0)),
            _resident((1, d)),
            _resident((d, 3 * d)),
            _resident((3, d)),
        ],
        out_specs=pl.BlockSpec((None, tm, d), lambda b, i: (b, i, 0)),
        scratch_shapes=[
            pltpu.VMEM((tm + 2 * halo, d), _BF16),
            pltpu.VMEM((tm + 2 * halo, d), _F32),
        ],
        compiler_params=_params("parallel", "parallel"),
        name="conv_mixer",
    )(h, h, h, g, w_in, k)


def _dft_tables(seq, gd):
    n1 = V7X_MXU_DIM // V7X_F32_SUBLANES
    n2 = seq // n1
    grp = V7X_BF16_SUBLANES
    two_pi = 2.0 * np.pi
    th1 = two_pi * np.outer(np.arange(n1), np.arange(n1)) / n1
    f1 = np.stack([np.cos(th1), -np.sin(th1)], axis=1).reshape(2 * n1, n1) / np.sqrt(n1)
    l1 = np.kron(f1, np.eye(V7X_F32_SUBLANES))
    th2 = two_pi * np.outer(np.arange(n2), np.arange(n2)) / n2
    c2 = np.cos(th2) / np.sqrt(n2)
    s2 = np.sin(th2) / np.sqrt(n2)
    tw = (two_pi * np.outer(np.arange(n1), np.arange(n2)) / seq).reshape(n1, 1, n2)
    ph = two_pi * np.outer(np.arange(gd), np.arange(gd)) / gd
    cs = np.concatenate([np.cos(ph), np.sin(ph)], axis=0) / np.sqrt(gd)
    perm = np.zeros((grp * grp, grp * grp))
    a, b = np.meshgrid(np.arange(grp), np.arange(grp), indexing="ij")
    perm[(b * grp + a).ravel(), (a * grp + b).ravel()] = 1.0
    f = lambda t: jnp.asarray(t, _F32)
    return n1, n2, f(l1), f(c2), f(s2), f(np.cos(tw)), f(np.sin(tw)), f(cs), f(perm)


def _fourier_stage1_kernel(h_ref, g_ref, l1_ref, t_ref, *, n1, jb, d):
    g = g_ref[...]
    l1 = l1_ref[...]
    sub = V7X_F32_SUBLANES
    for j in range(jb):
        halves = []
        for half in range(V7X_BF16_SUBLANES // sub):
            x = h_ref[:, j, half * sub:(half + 1) * sub, :].reshape(n1 * sub, d)
            xn = _rmsnorm(x, g).astype(_BF16)
            halves.append(_dot(l1, xn).reshape(2 * n1, sub, d))
        t = jnp.concatenate(halves, axis=1).astype(_BF16)
        t_ref[:, :, j] = t.reshape(n1, 2, V7X_BF16_SUBLANES, d)


def _fourier_stage1(h, g, l1, *, n1, n2, jb):
    bsz, seq, d = h.shape
    grp = V7X_BF16_SUBLANES
    ng = n2 // grp
    kern = functools.partial(_fourier_stage1_kernel, n1=n1, jb=jb, d=d)
    return pl.pallas_call(
        kern,
        out_shape=jax.ShapeDtypeStruct((bsz, n1, 2, ng, grp, d), _BF16),
        grid=(bsz, ng // jb),
        in_specs=[
            pl.BlockSpec((None, n1, jb, grp, d), lambda b, j: (b, 0, j, 0, 0)),
            _resident((1, d)),
            _resident(l1.shape),
        ],
        out_specs=pl.BlockSpec((None, n1, 2, jb, grp, d), lambda b, j: (b, 0, 0, j, 0, 0)),
        compiler_params=_params("parallel", "parallel"),
        name="fourier_stage1",
    )(h.reshape(bsz, n1, ng, grp, d), g, l1)


def _fourier_stage2_kernel(t_ref, c2_ref, s2_ref, twc_ref, tws_ref, cs_ref, p_ref, y_ref,
                           g_scr, ys_scr, *, kb, n2, d, gd):
    i = pl.program_id(2)
    grp = V7X_BF16_SUBLANES
    c2 = c2_ref[...]
    s2 = s2_ref[...]
    cs = cs_ref[...]
    for kk in range(kb):
        twc = twc_ref[kk]
        tws = tws_ref[kk]
        cth = (c2 * twc - s2 * tws).astype(_BF16)
        sth = s2 * twc + c2 * tws
        g_scr[0:n2, 0:n2] = cth
        g_scr[0:n2, n2:2 * n2] = sth.astype(_BF16)
        g_scr[n2:2 * n2, 0:n2] = (-sth).astype(_BF16)
        g_scr[n2:2 * n2, n2:2 * n2] = cth
        u = _dot(g_scr[...], t_ref[kk]).astype(_BF16)
        for gi in range(d // gd):
            gsl = slice(gi * gd, (gi + 1) * gd)
            lhs = jnp.concatenate([u[0:n2, gsl], u[n2:2 * n2, gsl]], axis=1)
            ys_scr[i * kb + kk, :, gsl] = _dot(lhs, cs).astype(_BF16)

    @pl.when(i == pl.num_programs(2) - 1)
    def _():
        p = p_ref[...]
        for k2h in range(n2 // grp):
            rows = slice(k2h * grp, (k2h + 1) * grp)
            stacked = jnp.concatenate([ys_scr[k1l, rows, :] for k1l in range(grp)], axis=0)
            y_ref[rows] = _dot(p, stacked).astype(_BF16).reshape(grp, grp, d)


def _fourier_stage2(t1, c2, s2, twc, tws, cs, perm, *, n1, n2, d, kb):
    bsz = t1.shape[0]
    gd = d // FOURIER_GROUPS
    grp = V7X_BF16_SUBLANES
    steps = grp // kb
    kern = functools.partial(_fourier_stage2_kernel, kb=kb, n2=n2, d=d, gd=gd)
    return pl.pallas_call(
        kern,
        out_shape=jax.ShapeDtypeStruct((bsz, n2, n1 // grp, grp, d), _BF16),
        grid=(bsz, n1 // grp, steps),
        in_specs=[
            pl.BlockSpec((None, kb, 2 * n2, d), lambda b, c, i: (b, c * steps + i, 0, 0)),
            _resident((n2, n2)),
            _resident((n2, n2)),
            pl.BlockSpec((kb, 1, n2), lambda b, c, i: (c * steps + i, 0, 0)),
            pl.BlockSpec((kb, 1, n2), lambda b, c, i: (c * steps + i, 0, 0)),
            _resident((2 * gd, gd)),
            _resident(perm.shape),
        ],
        out_specs=pl.BlockSpec((None, n2, None, grp, d), lambda b, c, i: (b, 0, c, 0, 0)),
        scratch_shapes=[
            pltpu.VMEM((2 * n2, 2 * n2), _BF16),
            pltpu.VMEM((grp, n2, d), _BF16),
        ],
        compiler_params=_params("parallel", "parallel", "arbitrary"),
        name="fourier_stage2",
    )(t1.reshape(bsz, n1, 2 * n2, d), c2, s2, twc, tws, cs, perm)


def _ffn_kernel(h_ref, m_ref, gn_ref, wo_ref, wg_ref, wu_ref, wd_ref, gf_ref, o_ref,
                hn_scr, act_scr, *, dff, fchunk, final):
    h1 = h_ref[...] + _dot(m_ref[...], wo_ref[...])
    o_ref[...] = h1
    hn_scr[...] = _rmsnorm(h1, gn_ref[...]).astype(_BF16)
    for c0 in range(0, dff, fchunk):
        a = _dot(hn_scr[...], wg_ref[:, c0:c0 + fchunk])
        b = _dot(hn_scr[...], wu_ref[:, c0:c0 + fchunk])
        act_scr[:, c0:c0 + fchunk] = (a * jax.nn.sigmoid(a) * b).astype(_BF16)
    out = o_ref[...] + _dot(act_scr[...], wd_ref[...])
    if final:
        out = _rmsnorm(out, gf_ref[...])
    o_ref[...] = out


def _ffn(h, m, gn, wo, wg, wu, wd, gf, *, tm, fchunk, final):
    n, d = h.shape
    dff = wg.shape[1]
    kern = functools.partial(_ffn_kernel, dff=dff, fchunk=fchunk, final=final)
    row = lambda i: (i, 0)
    return pl.pallas_call(
        kern,
        out_shape=jax.ShapeDtypeStruct((n, d), _F32),
        grid=(n // tm,),
        in_specs=[
            pl.BlockSpec((tm, d), row),
            pl.BlockSpec((tm, d), row),
            _resident((1, d)),
            _resident((d, d)),
            _resident((d, dff)),
            _resident((d, dff)),
            _resident((dff, d)),
            _resident((1, d)),
        ],
        out_specs=pl.BlockSpec((tm, d), row),
        scratch_shapes=[
            pltpu.VMEM((tm, d), _BF16),
            pltpu.VMEM((tm, dff), _BF16),
        ],
        compiler_params=_params("parallel"),
        name="outproj_swiglu",
    )(h, m, gn, wo, wg, wu, wd, gf)


def kernel(x, conv_w_in, conv_k, conv_w_out, fourier_w_out, mix_norm_g, ffn_norm_g,
           ffn_w_gate, ffn_w_up, ffn_w_down, final_norm_g):
    bsz, seq, d = x.shape
    depth = mix_norm_g.shape[0]
    dff = ffn_w_gate.shape[-1]

    tm_conv = 512
    tm_ffn = 512
    fchunk = 256
    assert seq % tm_conv == 0 and (bsz * seq) % tm_ffn == 0 and dff % fchunk == 0

    n1, n2, l1, c2, s2, twc, tws, cs, perm = _dft_tables(seq, d // FOURIER_GROUPS)
    l1, cs, perm = l1.astype(_BF16), cs.astype(_BF16), perm.astype(_BF16)
    jb = 2
    kb = 2
    assert d % FOURIER_GROUPS == 0 and seq == n1 * n2
    assert n2 % (V7X_BF16_SUBLANES * jb) == 0 and n1 % V7X_BF16_SUBLANES == 0

    bf = lambda w: w.astype(_BF16)
    conv_w_in, conv_w_out, fourier_w_out = bf(conv_w_in), bf(conv_w_out), bf(fourier_w_out)
    ffn_w_gate, ffn_w_up, ffn_w_down = bf(ffn_w_gate), bf(ffn_w_up), bf(ffn_w_down)
    gfin = final_norm_g.reshape(1, d)

    h = x
    for i in range(depth):
        j = i // 2
        gmix = mix_norm_g[i].reshape(1, d)
        if i % 2 == 0:
            m = _conv_mixer(h, gmix, conv_w_in[j], conv_k[j], tm=tm_conv)
            wo = conv_w_out[j]
        else:
            t1 = _fourier_stage1(h, gmix, l1, n1=n1, n2=n2, jb=jb)
            m = _fourier_stage2(t1, c2, s2, twc, tws, cs, perm, n1=n1, n2=n2, d=d, kb=kb)
            wo = fourier_w_out[j]
        h = _ffn(h.reshape(bsz * seq, d), m.reshape(bsz * seq, d),
                 ffn_norm_g[i].reshape(1, d), wo, ffn_w_gate[i], ffn_w_up[i], ffn_w_down[i],
                 gfin, tm=tm_ffn, fchunk=fchunk, final=(i == depth - 1)).reshape(bsz, seq, d)
    return h
```

```python
import functools

import numpy as np
import jax
import jax.numpy as jnp
from jax import lax
from jax.experimental import pallas as pl
from jax.experimental.pallas import tpu as pltpu

RMS_EPS = 1e-5
FOURIER_GROUPS = 8

V7X_LANES = 128
V7X_F32_SUBLANES = 8
V7X_BF16_SUBLANES = 16
V7X_MXU_DIM = 256
V7X_VMEM_LIMIT_BYTES = 56 * 1024 * 1024

_BF16 = jnp.bfloat16
_F32 = jnp.float32


def _dot(a, b):
    return jnp.dot(a, b, preferred_element_type=_F32)


def _rmsnorm(x, g):
    r = lax.rsqrt(jnp.mean(x * x, axis=-1, keepdims=True) + RMS_EPS)
    return x * r * g


def _resident(shape):
    return pl.BlockSpec(shape, lambda *_: (0,) * len(shape), pipeline_mode=pl.Buffered(1))


def _params(*semantics):
    return pltpu.CompilerParams(dimension_semantics=semantics,
                                vmem_limit_bytes=V7X_VMEM_LIMIT_BYTES)


def _cast_specs(stacked, layer, steps, step_of):
    _, rows, cols = stacked.shape
    nblk = steps
    while rows % (nblk * V7X_BF16_SUBLANES):
        nblk //= 2
    rb, every = rows // nblk, steps // nblk
    in_spec = pl.BlockSpec((None, rb, cols), lambda *g: (layer, step_of(*g) // every, 0))
    out_spec = pl.BlockSpec((rb, cols), lambda *g: (step_of(*g) // every, 0))
    return in_spec, out_spec, jax.ShapeDtypeStruct((rows, cols), _BF16)


def _with_casts(body, n_in, n_out, n_cast):
    def kern(*refs):
        ins, rest = refs[:n_in], refs[n_in:]
        cast_in, rest = rest[:n_cast], rest[n_cast:]
        outs, rest = rest[:n_out], rest[n_out:]
        cast_out, scratch = rest[:n_cast], rest[n_cast:]
        for src, dst in zip(cast_in, cast_out):
            dst[...] = src[...].astype(_BF16)
        body(*ins, *outs, *scratch)
    return kern


def _conv_mixer_kernel(h_ref, hprev_ref, hnext_ref, g_ref, win_ref, k_ref, m_ref,
                       xn_ref, u_ref, *, tm, halo, d):
    i = pl.program_id(1)
    g = g_ref[...]
    xn_ref[0:halo] = _rmsnorm(hprev_ref[...], g).astype(_BF16)
    xn_ref[halo:halo + tm] = _rmsnorm(h_ref[...], g).astype(_BF16)
    xn_ref[halo + tm:] = _rmsnorm(hnext_ref[...], g).astype(_BF16)

    c = _dot(xn_ref[...], win_ref[:, d:2 * d])
    v = _dot(xn_ref[...], win_ref[:, 2 * d:3 * d])
    u_ref[...] = c * v

    zeros = jnp.zeros((V7X_F32_SUBLANES, d), _F32)

    @pl.when(i == 0)
    def _():
        u_ref[halo - V7X_F32_SUBLANES:halo] = zeros

    @pl.when(i == pl.num_programs(1) - 1)
    def _():
        u_ref[halo + tm:halo + tm + V7X_F32_SUBLANES] = zeros

    conv = (k_ref[0:1] * u_ref[halo - 1:halo - 1 + tm]
            + k_ref[1:2] * u_ref[halo:halo + tm]
            + k_ref[2:3] * u_ref[halo + 1:halo + 1 + tm])
    b = _dot(xn_ref[halo:halo + tm], win_ref[:, 0:d])
    m_ref[...] = (b * conv).astype(_BF16)


def _conv_mixer(h, g, w_in, k, casts, *, tm):
    bsz, seq, d = h.shape
    halo = V7X_BF16_SUBLANES
    nt = seq // tm
    hb = tm // halo
    last = seq // halo - 1
    cspecs = [_cast_specs(w, l, bsz * nt, lambda b, i: b * nt + i) for w, l in casts]
    body = functools.partial(_conv_mixer_kernel, tm=tm, halo=halo, d=d)
    m, *cast_out = pl.pallas_call(
        _with_casts(body, 6, 1, len(casts)),
        out_shape=[jax.ShapeDtypeStruct((bsz, seq, d), _BF16)] + [c[2] for c in cspecs],
        grid=(bsz, nt),
        in_specs=[
            pl.BlockSpec((None, tm, d), lambda b, i: (b, i, 0)),
            pl.BlockSpec((None, halo, d), lambda b, i: (b, jnp.maximum(i * hb - 1, 0), 0)),
            pl.BlockSpec((None, halo, d), lambda b, i: (b, jnp.minimum((i + 1) * hb, last), 0)),
            _resident((1, d)),
            _resident((d, 3 * d)),
            _resident((3, d)),
        ] + [c[0] for c in cspecs],
        out_specs=[pl.BlockSpec((None, tm, d), lambda b, i: (b, i, 0))] + [c[1] for c in cspecs],
        scratch_shapes=[
            pltpu.VMEM((tm + 2 * halo, d), _BF16),
            pltpu.VMEM((tm + 2 * halo, d), _F32),
        ],
        compiler_params=_params("arbitrary", "arbitrary"),
        name="conv_mixer",
    )(h, h, h, g, w_in, k, *[w for w, _ in casts])
    return m, cast_out


def _dft_tables(seq, gd):
    n1 = V7X_MXU_DIM // V7X_F32_SUBLANES
    n2 = seq // n1
    grp = V7X_BF16_SUBLANES
    two_pi = 2.0 * np.pi
    th1 = two_pi * np.outer(np.arange(n1), np.arange(n1)) / n1
    f1 = np.stack([np.cos(th1), -np.sin(th1)], axis=1).reshape(2 * n1, n1) / np.sqrt(n1)
    l1 = np.kron(f1, np.eye(V7X_F32_SUBLANES))
    th2 = two_pi * np.outer(np.arange(n2), np.arange(n2)) / n2
    c2 = np.cos(th2) / np.sqrt(n2)
    s2 = np.sin(th2) / np.sqrt(n2)
    tw = (two_pi * np.outer(np.arange(n1), np.arange(n2)) / seq).reshape(n1, 1, n2)
    ph = two_pi * np.outer(np.arange(gd), np.arange(gd)) / gd
    cs = np.concatenate([np.cos(ph), np.sin(ph)], axis=0) / np.sqrt(gd)
    perm = np.zeros((grp * grp, grp * grp))
    a, b = np.meshgrid(np.arange(grp), np.arange(grp), indexing="ij")
    perm[(b * grp + a).ravel(), (a * grp + b).ravel()] = 1.0
    f = lambda t: jnp.asarray(t, _F32)
    return n1, n2, f(l1), f(c2), f(s2), f(np.cos(tw)), f(np.sin(tw)), f(cs), f(perm)


def _fourier_stage1_kernel(h_ref, g_ref, l1_ref, t_ref, *, n1, jb, d):
    g = g_ref[...]
    l1 = l1_ref[...]
    sub = V7X_F32_SUBLANES
    for j in range(jb):
        halves = []
        for half in range(V7X_BF16_SUBLANES // sub):
            x = h_ref[:, j, half * sub:(half + 1) * sub, :].reshape(n1 * sub, d)
            xn = _rmsnorm(x, g).astype(_BF16)
            halves.append(_dot(l1, xn).reshape(2 * n1, sub, d))
        t = jnp.concatenate(halves, axis=1).astype(_BF16)
        t_ref[:, :, j] = t.reshape(n1, 2, V7X_BF16_SUBLANES, d)


def _fourier_stage1(h, g, l1, *, n1, n2, jb):
    bsz, seq, d = h.shape
    grp = V7X_BF16_SUBLANES
    ng = n2 // grp
    kern = functools.partial(_fourier_stage1_kernel, n1=n1, jb=jb, d=d)
    return pl.pallas_call(
        kern,
        out_shape=jax.ShapeDtypeStruct((bsz, n1, 2, ng, grp, d), _BF16),
        grid=(bsz, ng // jb),
        in_specs=[
            pl.BlockSpec((None, n1, jb, grp, d), lambda b, j: (b, 0, j, 0, 0)),
            _resident((1, d)),
            _resident(l1.shape),
        ],
        out_specs=pl.BlockSpec((None, n1, 2, jb, grp, d), lambda b, j: (b, 0, 0, j, 0, 0)),
        compiler_params=_params("parallel", "parallel"),
        name="fourier_stage1",
    )(h.reshape(bsz, n1, ng, grp, d), g, l1)


def _fourier_stage2_kernel(t_ref, c2_ref, s2_ref, twc_ref, tws_ref, cs_ref, p_ref, y_ref,
                           g_scr, ys_scr, *, kb, n2, d, gd):
    i = pl.program_id(2)
    grp = V7X_BF16_SUBLANES
    c2 = c2_ref[...]
    s2 = s2_ref[...]
    cs = cs_ref[...]
    for kk in range(kb):
        twc = twc_ref[kk]
        tws = tws_ref[kk]
        cth = (c2 * twc - s2 * tws).astype(_BF16)
        sth = s2 * twc + c2 * tws
        g_scr[0:n2, 0:n2] = cth
        g_scr[0:n2, n2:2 * n2] = sth.astype(_BF16)
        g_scr[n2:2 * n2, 0:n2] = (-sth).astype(_BF16)
        g_scr[n2:2 * n2, n2:2 * n2] = cth
        u = _dot(g_scr[...], t_ref[kk]).astype(_BF16)
        for gi in range(d // gd):
            gsl = slice(gi * gd, (gi + 1) * gd)
            lhs = jnp.concatenate([u[0:n2, gsl], u[n2:2 * n2, gsl]], axis=1)
            ys_scr[i * kb + kk, :, gsl] = _dot(lhs, cs).astype(_BF16)

    @pl.when(i == pl.num_programs(2) - 1)
    def _():
        p = p_ref[...]
        for k2h in range(n2 // grp):
            rows = slice(k2h * grp, (k2h + 1) * grp)
            stacked = jnp.concatenate([ys_scr[k1l, rows, :] for k1l in range(grp)], axis=0)
            y_ref[rows] = _dot(p, stacked).astype(_BF16).reshape(grp, grp, d)


def _fourier_stage2(t1, c2, s2, twc, tws, cs, perm, *, n1, n2, d, kb):
    bsz = t1.shape[0]
    gd = d // FOURIER_GROUPS
    grp = V7X_BF16_SUBLANES
    steps = grp // kb
    kern = functools.partial(_fourier_stage2_kernel, kb=kb, n2=n2, d=d, gd=gd)
    return pl.pallas_call(
        kern,
        out_shape=jax.ShapeDtypeStruct((bsz, n2, n1 // grp, grp, d), _BF16),
        grid=(bsz, n1 // grp, steps),
        in_specs=[
            pl.BlockSpec((None, kb, 2 * n2, d), lambda b, c, i: (b, c * steps + i, 0, 0)),
            _resident((n2, n2)),
            _resident((n2, n2)),
            pl.BlockSpec((kb, 1, n2), lambda b, c, i: (c * steps + i, 0, 0)),
            pl.BlockSpec((kb, 1, n2), lambda b, c, i: (c * steps + i, 0, 0)),
            _resident((2 * gd, gd)),
            _resident(perm.shape),
        ],
        out_specs=pl.BlockSpec((None, n2, None, grp, d), lambda b, c, i: (b, 0, c, 0, 0)),
        scratch_shapes=[
            pltpu.VMEM((2 * n2, 2 * n2), _BF16),
            pltpu.VMEM((grp, n2, d), _BF16),
        ],
        compiler_params=_params("parallel", "parallel", "arbitrary"),
        name="fourier_stage2",
    )(t1.reshape(bsz, n1, 2 * n2, d), c2, s2, twc, tws, cs, perm)


def _ffn_kernel(h_ref, m_ref, gn_ref, wo_ref, wg_ref, wu_ref, wd_ref, gf_ref, o_ref,
                hn_scr, act_scr, *, dff, fchunk, final):
    h1 = h_ref[...] + _dot(m_ref[...], wo_ref[...])
    o_ref[...] = h1
    hn_scr[...] = _rmsnorm(h1, gn_ref[...]).astype(_BF16)
    for c0 in range(0, dff, fchunk):
        a = _dot(hn_scr[...], wg_ref[:, c0:c0 + fchunk])
        b = _dot(hn_scr[...], wu_ref[:, c0:c0 + fchunk])
        act_scr[:, c0:c0 + fchunk] = (a * jax.nn.sigmoid(a) * b).astype(_BF16)
    out = o_ref[...] + _dot(act_scr[...], wd_ref[...])
    if final:
        out = _rmsnorm(out, gf_ref[...])
    o_ref[...] = out


def _ffn(h, m, gn, wo, wg, wu, wd, gf, casts, *, tm, fchunk, final):
    n, d = h.shape
    dff = wg.shape[1]
    steps = n // tm
    cspecs = [_cast_specs(w, l, steps, lambda i: i) for w, l in casts]
    body = functools.partial(_ffn_kernel, dff=dff, fchunk=fchunk, final=final)
    row = lambda i: (i, 0)
    out, *cast_out = pl.pallas_call(
        _with_casts(body, 8, 1, len(casts)),
        out_shape=[jax.ShapeDtypeStruct((n, d), _F32)] + [c[2] for c in cspecs],
        grid=(steps,),
        in_specs=[
            pl.BlockSpec((tm, d), row),
            pl.BlockSpec((tm, d), row),
            _resident((1, d)),
            _resident((d, d)),
            _resident((d, dff)),
            _resident((d, dff)),
            _resident((dff, d)),
            _resident((1, d)),
        ] + [c[0] for c in cspecs],
        out_specs=[pl.BlockSpec((tm, d), row)] + [c[1] for c in cspecs],
        scratch_shapes=[
            pltpu.VMEM((tm, d), _BF16),
            pltpu.VMEM((tm, dff), _BF16),
        ],
        compiler_params=_params("arbitrary"),
        name="outproj_swiglu",
    )(h, m, gn, wo, wg, wu, wd, gf, *[w for w, _ in casts])
    return out, cast_out


def kernel(x, conv_w_in, conv_k, conv_w_out, fourier_w_out, mix_norm_g, ffn_norm_g,
           ffn_w_gate, ffn_w_up, ffn_w_down, final_norm_g):
    bsz, seq, d = x.shape
    depth = mix_norm_g.shape[0]
    dff = ffn_w_gate.shape[-1]

    tm_conv = 512
    tm_ffn = 512
    fchunk = 256
    assert seq % tm_conv == 0 and (bsz * seq) % tm_ffn == 0 and dff % fchunk == 0

    n1, n2, l1, c2, s2, twc, tws, cs, perm = _dft_tables(seq, d // FOURIER_GROUPS)
    l1, cs, perm = l1.astype(_BF16), cs.astype(_BF16), perm.astype(_BF16)
    jb = 2
    kb = 2
    assert d % FOURIER_GROUPS == 0 and seq == n1 * n2
    assert n2 % (V7X_BF16_SUBLANES * jb) == 0 and n1 % V7X_BF16_SUBLANES == 0

    gfin = final_norm_g.reshape(1, d)
    is_conv = lambda i: i % 2 == 0

    def ffn_casts(i):
        wo = (conv_w_out, i // 2) if is_conv(i) else (fourier_w_out, i // 2)
        return [wo, (ffn_w_gate, i), (ffn_w_up, i), (ffn_w_down, i)]

    def next_layer_casts(i):
        if i + 1 >= depth:
            return []
        return ([(conv_w_in, (i + 1) // 2)] if is_conv(i + 1) else []) + ffn_casts(i + 1)

    w_in = conv_w_in[0].astype(_BF16)
    ffn_w = None
    h = x
    for i in range(depth):
        j = i // 2
        gmix = mix_norm_g[i].reshape(1, d)
        if is_conv(i):
            m, cast = _conv_mixer(h, gmix, w_in, conv_k[j], ffn_casts(i) if ffn_w is None else [],
                                  tm=tm_conv)
            ffn_w = cast or ffn_w
        else:
            t1 = _fourier_stage1(h, gmix, l1, n1=n1, n2=n2, jb=jb)
            m = _fourier_stage2(t1, c2, s2, twc, tws, cs, perm, n1=n1, n2=n2, d=d, kb=kb)
        h, cast = _ffn(h.reshape(bsz * seq, d), m.reshape(bsz * seq, d),
                       ffn_norm_g[i].reshape(1, d), *ffn_w, gfin, next_layer_casts(i),
                       tm=tm_ffn, fchunk=fchunk, final=(i == depth - 1))
        h = h.reshape(bsz, seq, d)
        if i + 1 < depth and is_conv(i + 1):
            w_in, *cast = cast
        ffn_w = cast
    return h
```

```python
import functools

import numpy as np
import jax
import jax.numpy as jnp
from jax import lax
from jax.experimental import pallas as pl
from jax.experimental.pallas import tpu as pltpu

RMS_EPS = 1e-5
FOURIER_GROUPS = 8

V7X_LANES = 128
V7X_F32_SUBLANES = 8
V7X_BF16_SUBLANES = 16
V7X_MXU_DIM = 256
V7X_VMEM_LIMIT_BYTES = 56 * 1024 * 1024

_BF16 = jnp.bfloat16
_F32 = jnp.float32


def _dot(a, b):
    return jnp.dot(a, b, preferred_element_type=_F32)


def _rmsnorm(x, g):
    r = lax.rsqrt(jnp.mean(x * x, axis=-1, keepdims=True) + RMS_EPS)
    return x * r * g


def _resident(shape):
    return pl.BlockSpec(shape, lambda *_: (0,) * len(shape), pipeline_mode=pl.Buffered(1))


def _params(*semantics):
    return pltpu.CompilerParams(dimension_semantics=semantics,
                                vmem_limit_bytes=V7X_VMEM_LIMIT_BYTES)


def _cast_specs(stacked, layer, steps, step_of):
    _, rows, cols = stacked.shape
    nblk = steps
    while rows % (nblk * V7X_BF16_SUBLANES):
        nblk //= 2
    rb, every = rows // nblk, steps // nblk
    in_spec = pl.BlockSpec((None, rb, cols), lambda *g: (layer, step_of(*g) // every, 0))
    out_spec = pl.BlockSpec((rb, cols), lambda *g: (step_of(*g) // every, 0))
    return in_spec, out_spec, jax.ShapeDtypeStruct((rows, cols), _BF16)


def _with_casts(body, n_in, n_out, n_cast):
    def kern(*refs):
        ins, rest = refs[:n_in], refs[n_in:]
        cast_in, rest = rest[:n_cast], rest[n_cast:]
        outs, rest = rest[:n_out], rest[n_out:]
        cast_out, scratch = rest[:n_cast], rest[n_cast:]
        for src, dst in zip(cast_in, cast_out):
            dst[...] = src[...].astype(_BF16)
        body(*ins, *outs, *scratch)
    return kern


def _conv_mixer_kernel(h_ref, hprev_ref, hnext_ref, g_ref, win_ref, k_ref, m_ref,
                       xn_ref, *, tm, nsub, d):
    halo = V7X_F32_SUBLANES
    i = pl.program_id(1)
    g = g_ref[...]
    cs = tm // nsub
    xn_ref[0:halo] = _rmsnorm(hnext_ref[...], g).astype(_BF16)
    xn_ref[halo:2 * halo] = _rmsnorm(hprev_ref[...], g).astype(_BF16)
    for q in range(nsub):
        xn_ref[2 * halo + q * cs:2 * halo + (q + 1) * cs] = (
            _rmsnorm(h_ref[q * cs:(q + 1) * cs], g).astype(_BF16))

    starts = [0] + [2 * halo + q * cs for q in range(1, nsub)]
    u = []
    for q in range(nsub):
        x = xn_ref[starts[q]:2 * halo + (q + 1) * cs]
        u.append(_dot(x, win_ref[:, d:2 * d]) * _dot(x, win_ref[:, 2 * d:3 * d]))
    u_next = jnp.where(i == pl.num_programs(1) - 1, 0.0, u[0][0:halo])
    u_prev = jnp.where(i == 0, 0.0, u[0][halo:2 * halo])
    u[0] = u[0][2 * halo:]

    for q in range(nsub):
        before = u_prev if q == 0 else u[q - 1][cs - halo:]
        after = u_next if q == nsub - 1 else u[q + 1][0:halo]
        slab = jnp.concatenate([before, u[q], after], axis=0)
        up = pltpu.roll(slab, 1, 0)[halo:halo + cs]
        dn = pltpu.roll(slab, cs + 2 * halo - 1, 0)[halo:halo + cs]
        conv = k_ref[0:1] * up + k_ref[1:2] * u[q] + k_ref[2:3] * dn
        rows = slice(2 * halo + q * cs, 2 * halo + (q + 1) * cs)
        b = _dot(xn_ref[rows], win_ref[:, 0:d])
        m_ref[q * cs:(q + 1) * cs] = (b * conv).astype(_BF16)


def _conv_mixer(h, g, w_in, k, casts, *, tm, nsub):
    bsz, seq, d = h.shape
    halo = V7X_F32_SUBLANES
    nt = seq // tm
    hb = tm // halo
    last = seq // halo - 1
    cspecs = [_cast_specs(w, l, bsz * nt, lambda b, i: b * nt + i) for w, l in casts]
    body = functools.partial(_conv_mixer_kernel, tm=tm, nsub=nsub, d=d)
    m, *cast_out = pl.pallas_call(
        _with_casts(body, 6, 1, len(casts)),
        out_shape=[jax.ShapeDtypeStruct((bsz, seq, d), _BF16)] + [c[2] for c in cspecs],
        grid=(bsz, nt),
        in_specs=[
            pl.BlockSpec((None, tm, d), lambda b, i: (b, i, 0)),
            pl.BlockSpec((None, halo, d), lambda b, i: (b, jnp.maximum(i * hb - 1, 0), 0)),
            pl.BlockSpec((None, halo, d), lambda b, i: (b, jnp.minimum((i + 1) * hb, last), 0)),
            _resident((1, d)),
            _resident((d, 3 * d)),
            _resident((3, d)),
        ] + [c[0] for c in cspecs],
        out_specs=[pl.BlockSpec((None, tm, d), lambda b, i: (b, i, 0))] + [c[1] for c in cspecs],
        scratch_shapes=[pltpu.VMEM((tm + 2 * halo, d), _BF16)],
        compiler_params=_params("arbitrary", "arbitrary"),
        name="conv_mixer",
    )(h, h, h, g, w_in, k, *[w for w, _ in casts])
    return m, cast_out


def _dft_tables(seq, gd):
    n1 = V7X_MXU_DIM // V7X_F32_SUBLANES
    n2 = seq // n1
    grp = V7X_BF16_SUBLANES
    two_pi = 2.0 * np.pi
    th1 = two_pi * np.outer(np.arange(n1), np.arange(n1)) / n1
    f1 = np.stack([np.cos(th1), -np.sin(th1)], axis=1).reshape(2 * n1, n1) / np.sqrt(n1)
    l1 = np.kron(f1, np.eye(V7X_F32_SUBLANES))
    th2 = two_pi * np.outer(np.arange(n2), np.arange(n2)) / n2
    c2 = np.cos(th2) / np.sqrt(n2)
    s2 = np.sin(th2) / np.sqrt(n2)
    tw = (two_pi * np.outer(np.arange(n1), np.arange(n2)) / seq).reshape(n1, 1, n2)
    ph = two_pi * np.outer(np.arange(gd), np.arange(gd)) / gd
    cs = np.concatenate([np.cos(ph), np.sin(ph)], axis=0) / np.sqrt(gd)
    perm = np.zeros((grp * grp, grp * grp))
    a, b = np.meshgrid(np.arange(grp), np.arange(grp), indexing="ij")
    perm[(b * grp + a).ravel(), (a * grp + b).ravel()] = 1.0
    f = lambda t: jnp.asarray(t, _F32)
    return n1, n2, f(l1), f(c2), f(s2), f(np.cos(tw)), f(np.sin(tw)), f(cs), f(perm)


def _fourier_stage1_kernel(h_ref, g_ref, l1_ref, t_ref, *, n1, jb, d):
    g = g_ref[...]
    l1 = l1_ref[...]
    sub = V7X_F32_SUBLANES
    for j in range(jb):
        halves = []
        for half in range(V7X_BF16_SUBLANES // sub):
            x = h_ref[:, j, half * sub:(half + 1) * sub, :].reshape(n1 * sub, d)
            xn = _rmsnorm(x, g).astype(_BF16)
            halves.append(_dot(l1, xn).reshape(2 * n1, sub, d))
        t = jnp.concatenate(halves, axis=1).astype(_BF16)
        t_ref[:, :, j] = t.reshape(n1, 2, V7X_BF16_SUBLANES, d)


def _fourier_stage1(h, g, l1, *, n1, n2, jb):
    bsz, seq, d = h.shape
    grp = V7X_BF16_SUBLANES
    ng = n2 // grp
    kern = functools.partial(_fourier_stage1_kernel, n1=n1, jb=jb, d=d)
    return pl.pallas_call(
        kern,
        out_shape=jax.ShapeDtypeStruct((bsz, n1, 2, ng, grp, d), _BF16),
        grid=(bsz, ng // jb),
        in_specs=[
            pl.BlockSpec((None, n1, jb, grp, d), lambda b, j: (b, 0, j, 0, 0)),
            _resident((1, d)),
            _resident(l1.shape),
        ],
        out_specs=pl.BlockSpec((None, n1, 2, jb, grp, d), lambda b, j: (b, 0, 0, j, 0, 0)),
        compiler_params=_params("parallel", "parallel"),
        name="fourier_stage1",
    )(h.reshape(bsz, n1, ng, grp, d), g, l1)


def _fourier_stage2_kernel(t_ref, c2_ref, s2_ref, twc_ref, tws_ref, cs_ref, p_ref, y_ref,
                           g_scr, ys_scr, *, kb, n2, d, gd):
    i = pl.program_id(2)
    grp = V7X_BF16_SUBLANES
    c2 = c2_ref[...]
    s2 = s2_ref[...]
    cs = cs_ref[...]
    for kk in range(kb):
        twc = twc_ref[kk]
        tws = tws_ref[kk]
        cth = (c2 * twc - s2 * tws).astype(_BF16)
        sth = s2 * twc + c2 * tws
        g_scr[0:n2, 0:n2] = cth
        g_scr[0:n2, n2:2 * n2] = sth.astype(_BF16)
        g_scr[n2:2 * n2, 0:n2] = (-sth).astype(_BF16)
        g_scr[n2:2 * n2, n2:2 * n2] = cth
        u = _dot(g_scr[...], t_ref[kk]).astype(_BF16)
        for gi in range(d // gd):
            gsl = slice(gi * gd, (gi + 1) * gd)
            lhs = jnp.concatenate([u[0:n2, gsl], u[n2:2 * n2, gsl]], axis=1)
            ys_scr[i * kb + kk, :, gsl] = _dot(lhs, cs).astype(_BF16)

    @pl.when(i == pl.num_programs(2) - 1)
    def _():
        p = p_ref[...]
        for k2h in range(n2 // grp):
            rows = slice(k2h * grp, (k2h + 1) * grp)
            stacked = jnp.concatenate([ys_scr[k1l, rows, :] for k1l in range(grp)], axis=0)
            y_ref[rows] = _dot(p, stacked).astype(_BF16).reshape(grp, grp, d)


def _fourier_stage2(t1, c2, s2, twc, tws, cs, perm, *, n1, n2, d, kb):
    bsz = t1.shape[0]
    gd = d // FOURIER_GROUPS
    grp = V7X_BF16_SUBLANES
    steps = grp // kb
    kern = functools.partial(_fourier_stage2_kernel, kb=kb, n2=n2, d=d, gd=gd)
    return pl.pallas_call(
        kern,
        out_shape=jax.ShapeDtypeStruct((bsz, n2, n1 // grp, grp, d), _BF16),
        grid=(bsz, n1 // grp, steps),
        in_specs=[
            pl.BlockSpec((None, kb, 2 * n2, d), lambda b, c, i: (b, c * steps + i, 0, 0)),
            _resident((n2, n2)),
            _resident((n2, n2)),
            pl.BlockSpec((kb, 1, n2), lambda b, c, i: (c * steps + i, 0, 0)),
            pl.BlockSpec((kb, 1, n2), lambda b, c, i: (c * steps + i, 0, 0)),
            _resident((2 * gd, gd)),
            _resident(perm.shape),
        ],
        out_specs=pl.BlockSpec((None, n2, None, grp, d), lambda b, c, i: (b, 0, c, 0, 0)),
        scratch_shapes=[
            pltpu.VMEM((2 * n2, 2 * n2), _BF16),
            pltpu.VMEM((grp, n2, d), _BF16),
        ],
        compiler_params=_params("parallel", "parallel", "arbitrary"),
        name="fourier_stage2",
    )(t1.reshape(bsz, n1, 2 * n2, d), c2, s2, twc, tws, cs, perm)


def _ffn_kernel(h_ref, m_ref, gn_ref, wo_ref, wg_ref, wu_ref, wd_ref, gf_ref, o_ref,
                hn_scr, act_scr, *, dff, fchunk, final):
    h1 = h_ref[...] + _dot(m_ref[...], wo_ref[...])
    o_ref[...] = h1
    hn_scr[...] = _rmsnorm(h1, gn_ref[...]).astype(_BF16)
    for c0 in range(0, dff, fchunk):
        a = _dot(hn_scr[...], wg_ref[:, c0:c0 + fchunk])
        b = _dot(hn_scr[...], wu_ref[:, c0:c0 + fchunk])
        act_scr[:, c0:c0 + fchunk] = (a * jax.nn.sigmoid(a) * b).astype(_BF16)
    out = o_ref[...] + _dot(act_scr[...], wd_ref[...])
    if final:
        out = _rmsnorm(out, gf_ref[...])
    o_ref[...] = out


def _ffn(h, m, gn, wo, wg, wu, wd, gf, casts, *, tm, fchunk, final):
    n, d = h.shape
    dff = wg.shape[1]
    steps = n // tm
    cspecs = [_cast_specs(w, l, steps, lambda i: i) for w, l in casts]
    body = functools.partial(_ffn_kernel, dff=dff, fchunk=fchunk, final=final)
    row = lambda i: (i, 0)
    out, *cast_out = pl.pallas_call(
        _with_casts(body, 8, 1, len(casts)),
        out_shape=[jax.ShapeDtypeStruct((n, d), _F32)] + [c[2] for c in cspecs],
        grid=(steps,),
        in_specs=[
            pl.BlockSpec((tm, d), row),
            pl.BlockSpec((tm, d), row),
            _resident((1, d)),
            _resident((d, d)),
            _resident((d, dff)),
            _resident((d, dff)),
            _resident((dff, d)),
            _resident((1, d)),
        ] + [c[0] for c in cspecs],
        out_specs=[pl.BlockSpec((tm, d), row)] + [c[1] for c in cspecs],
        scratch_shapes=[
            pltpu.VMEM((tm, d), _BF16),
            pltpu.VMEM((tm, dff), _BF16),
        ],
        compiler_params=_params("arbitrary"),
        name="outproj_swiglu",
    )(h, m, gn, wo, wg, wu, wd, gf, *[w for w, _ in casts])
    return out, cast_out


def kernel(x, conv_w_in, conv_k, conv_w_out, fourier_w_out, mix_norm_g, ffn_norm_g,
           ffn_w_gate, ffn_w_up, ffn_w_down, final_norm_g):
    bsz, seq, d = x.shape
    depth = mix_norm_g.shape[0]
    dff = ffn_w_gate.shape[-1]

    tm_conv = 512
    tm_ffn = 512
    fchunk = 256
    assert seq % tm_conv == 0 and (bsz * seq) % tm_ffn == 0 and dff % fchunk == 0

    n1, n2, l1, c2, s2, twc, tws, cs, perm = _dft_tables(seq, d // FOURIER_GROUPS)
    l1, cs, perm = l1.astype(_BF16), cs.astype(_BF16), perm.astype(_BF16)
    jb = 2
    kb = 2
    assert d % FOURIER_GROUPS == 0 and seq == n1 * n2
    assert n2 % (V7X_BF16_SUBLANES * jb) == 0 and n1 % V7X_BF16_SUBLANES == 0

    gfin = final_norm_g.reshape(1, d)
    is_conv = lambda i: i % 2 == 0

    def ffn_casts(i):
        wo = (conv_w_out, i // 2) if is_conv(i) else (fourier_w_out, i // 2)
        return [wo, (ffn_w_gate, i), (ffn_w_up, i), (ffn_w_down, i)]

    def next_layer_casts(i):
        if i + 1 >= depth:
            return []
        return ([(conv_w_in, (i + 1) // 2)] if is_conv(i + 1) else []) + ffn_casts(i + 1)

    w_in = conv_w_in[0].astype(_BF16)
    ffn_w = None
    h = x
    for i in range(depth):
        j = i // 2
        gmix = mix_norm_g[i].reshape(1, d)
        if is_conv(i):
            m, cast = _conv_mixer(h, gmix, w_in, conv_k[j], ffn_casts(i) if ffn_w is None else [],
                                  tm=tm_conv, nsub=2)
            ffn_w = cast or ffn_w
        else:
            t1 = _fourier_stage1(h, gmix, l1, n1=n1, n2=n2, jb=jb)
            m = _fourier_stage2(t1, c2, s2, twc, tws, cs, perm, n1=n1, n2=n2, d=d, kb=kb)
        h, cast = _ffn(h.reshape(bsz * seq, d), m.reshape(bsz * seq, d),
                       ffn_norm_g[i].reshape(1, d), *ffn_w, gfin, next_layer_casts(i),
                       tm=tm_ffn, fchunk=fchunk, final=(i == depth - 1))
        h = h.reshape(bsz, seq, d)
        if i + 1 < depth and is_conv(i + 1):
            w_in, *cast = cast
        ffn_w = cast
    return h
```

```python
import functools

import numpy as np
import jax
import jax.numpy as jnp
from jax import lax
from jax.experimental import pallas as pl
from jax.experimental.pallas import tpu as pltpu

RMS_EPS = 1e-5
FOURIER_GROUPS = 8

V7X_LANES = 128
V7X_F32_SUBLANES = 8
V7X_BF16_SUBLANES = 16
V7X_MXU_DIM = 256
V7X_VMEM_LIMIT_BYTES = 56 * 1024 * 1024

_BF16 = jnp.bfloat16
_F32 = jnp.float32


def _dot(a, b):
    return jnp.dot(a, b, preferred_element_type=_F32)


def _rmsnorm(x, g):
    r = lax.rsqrt(jnp.mean(x * x, axis=-1, keepdims=True) + RMS_EPS)
    return x * r * g


def _resident(shape):
    return pl.BlockSpec(shape, lambda *_: (0,) * len(shape), pipeline_mode=pl.Buffered(1))


def _params(*semantics):
    return pltpu.CompilerParams(dimension_semantics=semantics,
                                vmem_limit_bytes=V7X_VMEM_LIMIT_BYTES)


def _cast_specs(stacked, layer, steps, step_of):
    _, rows, cols = stacked.shape
    nblk = steps
    while rows % (nblk * V7X_BF16_SUBLANES):
        nblk //= 2
    rb, every = rows // nblk, steps // nblk
    in_spec = pl.BlockSpec((None, rb, cols), lambda *g: (layer, step_of(*g) // every, 0))
    out_spec = pl.BlockSpec((rb, cols), lambda *g: (step_of(*g) // every, 0))
    return in_spec, out_spec, jax.ShapeDtypeStruct((rows, cols), _BF16)


def _with_casts(body, n_in, n_out, n_cast):
    def kern(*refs):
        ins, rest = refs[:n_in], refs[n_in:]
        cast_in, rest = rest[:n_cast], rest[n_cast:]
        outs, rest = rest[:n_out], rest[n_out:]
        cast_out, scratch = rest[:n_cast], rest[n_cast:]
        for src, dst in zip(cast_in, cast_out):
            dst[...] = src[...].astype(_BF16)
        body(*ins, *outs, *scratch)
    return kern


def _conv_mixer_kernel(h_ref, hprev_ref, hnext_ref, g_ref, win_ref, k_ref, m_ref,
                       xn_ref, *, tm, nsub, d):
    halo = V7X_F32_SUBLANES
    i = pl.program_id(1)
    g = g_ref[...]
    cs = tm // nsub
    xn_ref[0:halo] = _rmsnorm(hnext_ref[...], g).astype(_BF16)
    xn_ref[halo:2 * halo] = _rmsnorm(hprev_ref[...], g).astype(_BF16)
    for q in range(nsub):
        xn_ref[2 * halo + q * cs:2 * halo + (q + 1) * cs] = (
            _rmsnorm(h_ref[q * cs:(q + 1) * cs], g).astype(_BF16))

    starts = [0] + [2 * halo + q * cs for q in range(1, nsub)]
    u = []
    for q in range(nsub):
        x = xn_ref[starts[q]:2 * halo + (q + 1) * cs]
        u.append(_dot(x, win_ref[:, d:2 * d]) * _dot(x, win_ref[:, 2 * d:3 * d]))
    u_next = jnp.where(i == pl.num_programs(1) - 1, 0.0, u[0][0:halo])
    u_prev = jnp.where(i == 0, 0.0, u[0][halo:2 * halo])
    u[0] = u[0][2 * halo:]

    for q in range(nsub):
        before = u_prev if q == 0 else u[q - 1][cs - halo:]
        after = u_next if q == nsub - 1 else u[q + 1][0:halo]
        slab = jnp.concatenate([before, u[q], after], axis=0)
        up = pltpu.roll(slab, 1, 0)[halo:halo + cs]
        dn = pltpu.roll(slab, cs + 2 * halo - 1, 0)[halo:halo + cs]
        conv = k_ref[0:1] * up + k_ref[1:2] * u[q] + k_ref[2:3] * dn
        rows = slice(2 * halo + q * cs, 2 * halo + (q + 1) * cs)
        b = _dot(xn_ref[rows], win_ref[:, 0:d])
        m_ref[q * cs:(q + 1) * cs] = (b * conv).astype(_BF16)


def _conv_mixer(h, g, w_in, k, casts, *, tm, nsub):
    bsz, seq, d = h.shape
    halo = V7X_F32_SUBLANES
    nt = seq // tm
    hb = tm // halo
    last = seq // halo - 1
    cspecs = [_cast_specs(w, l, bsz * nt, lambda b, i: b * nt + i) for w, l in casts]
    body = functools.partial(_conv_mixer_kernel, tm=tm, nsub=nsub, d=d)
    m, *cast_out = pl.pallas_call(
        _with_casts(body, 6, 1, len(casts)),
        out_shape=[jax.ShapeDtypeStruct((bsz, seq, d), _BF16)] + [c[2] for c in cspecs],
        grid=(bsz, nt),
        in_specs=[
            pl.BlockSpec((None, tm, d), lambda b, i: (b, i, 0)),
            pl.BlockSpec((None, halo, d), lambda b, i: (b, jnp.maximum(i * hb - 1, 0), 0)),
            pl.BlockSpec((None, halo, d), lambda b, i: (b, jnp.minimum((i + 1) * hb, last), 0)),
            _resident((1, d)),
            _resident((d, 3 * d)),
            _resident((3, d)),
        ] + [c[0] for c in cspecs],
        out_specs=[pl.BlockSpec((None, tm, d), lambda b, i: (b, i, 0))] + [c[1] for c in cspecs],
        scratch_shapes=[pltpu.VMEM((tm + 2 * halo, d), _BF16)],
        compiler_params=_params("arbitrary", "arbitrary"),
        name="conv_mixer",
    )(h, h, h, g, w_in, k, *[w for w, _ in casts])
    return m, cast_out


def _dft_tables(seq, gd):
    n1 = V7X_MXU_DIM // V7X_F32_SUBLANES
    n2 = seq // n1
    grp = V7X_BF16_SUBLANES
    two_pi = 2.0 * np.pi
    th1 = two_pi * np.outer(np.arange(n1), np.arange(n1)) / n1
    f1 = np.stack([np.cos(th1), -np.sin(th1)], axis=1).reshape(2 * n1, n1) / np.sqrt(n1)
    l1 = np.kron(f1, np.eye(V7X_F32_SUBLANES))
    th2 = two_pi * np.outer(np.arange(n2), np.arange(n2)) / n2
    c2 = np.cos(th2) / np.sqrt(n2)
    s2 = np.sin(th2) / np.sqrt(n2)
    tw = (two_pi * np.outer(np.arange(n1), np.arange(n2)) / seq).reshape(n1, 1, n2)
    ph = two_pi * np.outer(np.arange(gd), np.arange(gd)) / gd
    cs = np.concatenate([np.cos(ph), np.sin(ph)], axis=0) / np.sqrt(gd)
    perm = np.zeros((grp * grp, grp * grp))
    a, b = np.meshgrid(np.arange(grp), np.arange(grp), indexing="ij")
    perm[(b * grp + a).ravel(), (a * grp + b).ravel()] = 1.0
    f = lambda t: jnp.asarray(t, _F32)
    return n1, n2, f(l1), f(c2), f(s2), f(np.cos(tw)), f(np.sin(tw)), f(cs), f(perm)


def _fourier_stage1_kernel(h_ref, g_ref, l1_ref, t_ref, *, n1, jb, d):
    g = g_ref[...]
    l1 = l1_ref[...]
    sub = V7X_F32_SUBLANES
    for j in range(jb):
        halves = []
        for half in range(V7X_BF16_SUBLANES // sub):
            x = h_ref[:, j, half * sub:(half + 1) * sub, :].reshape(n1 * sub, d)
            xn = _rmsnorm(x, g).astype(_BF16)
            halves.append(_dot(l1, xn).reshape(2 * n1, sub, d))
        t = jnp.concatenate(halves, axis=1).astype(_BF16)
        t_ref[:, :, j] = t.reshape(n1, 2, V7X_BF16_SUBLANES, d)


def _fourier_stage1(h, g, l1, *, n1, n2, jb):
    bsz, seq, d = h.shape
    grp = V7X_BF16_SUBLANES
    ng = n2 // grp
    kern = functools.partial(_fourier_stage1_kernel, n1=n1, jb=jb, d=d)
    return pl.pallas_call(
        kern,
        out_shape=jax.ShapeDtypeStruct((bsz, n1, 2, ng, grp, d), _BF16),
        grid=(bsz, ng // jb),
        in_specs=[
            pl.BlockSpec((None, n1, jb, grp, d), lambda b, j: (b, 0, j, 0, 0)),
            _resident((1, d)),
            _resident(l1.shape),
        ],
        out_specs=pl.BlockSpec((None, n1, 2, jb, grp, d), lambda b, j: (b, 0, 0, j, 0, 0)),
        compiler_params=_params("parallel", "parallel"),
        name="fourier_stage1",
    )(h.reshape(bsz, n1, ng, grp, d), g, l1)


def _fourier_stage2_kernel(t_ref, c2_ref, s2_ref, twc_ref, tws_ref, cs_ref, p_ref, y_ref,
                           g_scr, ys_scr, *, kb, n2, d, gd):
    i = pl.program_id(2)
    grp = V7X_BF16_SUBLANES
    c2 = c2_ref[...]
    s2 = s2_ref[...]
    cs = cs_ref[...]
    for kk in range(kb):
        twc = twc_ref[kk]
        tws = tws_ref[kk]
        cth = (c2 * twc - s2 * tws).astype(_BF16)
        sth = s2 * twc + c2 * tws
        g_scr[0:n2, 0:n2] = cth
        g_scr[0:n2, n2:2 * n2] = sth.astype(_BF16)
        g_scr[n2:2 * n2, 0:n2] = (-sth).astype(_BF16)
        g_scr[n2:2 * n2, n2:2 * n2] = cth
        u = _dot(g_scr[...], t_ref[kk]).astype(_BF16)
        for gi in range(d // gd):
            gsl = slice(gi * gd, (gi + 1) * gd)
            lhs = jnp.concatenate([u[0:n2, gsl], u[n2:2 * n2, gsl]], axis=1)
            ys_scr[i * kb + kk, :, gsl] = _dot(lhs, cs).astype(_BF16)

    @pl.when(i == pl.num_programs(2) - 1)
    def _():
        p = p_ref[...]
        for k2h in range(n2 // grp):
            rows = slice(k2h * grp, (k2h + 1) * grp)
            stacked = jnp.concatenate([ys_scr[k1l, rows, :] for k1l in range(grp)], axis=0)
            y_ref[rows] = _dot(p, stacked).astype(_BF16).reshape(grp, grp, d)


def _fourier_stage2(t1, c2, s2, twc, tws, cs, perm, *, n1, n2, d, kb):
    bsz = t1.shape[0]
    gd = d // FOURIER_GROUPS
    grp = V7X_BF16_SUBLANES
    steps = grp // kb
    kern = functools.partial(_fourier_stage2_kernel, kb=kb, n2=n2, d=d, gd=gd)
    return pl.pallas_call(
        kern,
        out_shape=jax.ShapeDtypeStruct((bsz, n2, n1 // grp, grp, d), _BF16),
        grid=(bsz, n1 // grp, steps),
        in_specs=[
            pl.BlockSpec((None, kb, 2 * n2, d), lambda b, c, i: (b, c * steps + i, 0, 0)),
            _resident((n2, n2)),
            _resident((n2, n2)),
            pl.BlockSpec((kb, 1, n2), lambda b, c, i: (c * steps + i, 0, 0)),
            pl.BlockSpec((kb, 1, n2), lambda b, c, i: (c * steps + i, 0, 0)),
            _resident((2 * gd, gd)),
            _resident(perm.shape),
        ],
        out_specs=pl.BlockSpec((None, n2, None, grp, d), lambda b, c, i: (b, 0, c, 0, 0)),
        scratch_shapes=[
            pltpu.VMEM((2 * n2, 2 * n2), _BF16),
            pltpu.VMEM((grp, n2, d), _BF16),
        ],
        compiler_params=_params("parallel", "parallel", "arbitrary"),
        name="fourier_stage2",
    )(t1.reshape(bsz, n1, 2 * n2, d), c2, s2, twc, tws, cs, perm)


def _ffn_kernel(h_ref, m_ref, gn_ref, wo_ref, wg_ref, wu_ref, wd_ref, gf_ref, o_ref,
                hn_scr, act_scr, *, dff, fchunk, final):
    h1 = h_ref[...] + _dot(m_ref[...], wo_ref[...])
    o_ref[...] = h1
    hn_scr[...] = _rmsnorm(h1, gn_ref[...]).astype(_BF16)
    for c0 in range(0, dff, fchunk):
        a = _dot(hn_scr[...], wg_ref[:, c0:c0 + fchunk])
        b = _dot(hn_scr[...], wu_ref[:, c0:c0 + fchunk])
        act_scr[:, c0:c0 + fchunk] = (a * jax.nn.sigmoid(a) * b).astype(_BF16)
    out = o_ref[...] + _dot(act_scr[...], wd_ref[...])
    if final:
        out = _rmsnorm(out, gf_ref[...])
    o_ref[...] = out


def _ffn(h, m, gn, wo, wg, wu, wd, gf, casts, *, tm, fchunk, final):
    n, d = h.shape
    dff = wg.shape[1]
    steps = n // tm
    cspecs = [_cast_specs(w, l, steps, lambda i: i) for w, l in casts]
    body = functools.partial(_ffn_kernel, dff=dff, fchunk=fchunk, final=final)
    row = lambda i: (i, 0)
    out, *cast_out = pl.pallas_call(
        _with_casts(body, 8, 1, len(casts)),
        out_shape=[jax.ShapeDtypeStruct((n, d), _F32)] + [c[2] for c in cspecs],
        grid=(steps,),
        in_specs=[
            pl.BlockSpec((tm, d), row),
            pl.BlockSpec((tm, d), row),
            _resident((1, d)),
            _resident((d, d)),
            _resident((d, dff)),
            _resident((d, dff)),
            _resident((dff, d)),
            _resident((1, d)),
        ] + [c[0] for c in cspecs],
        out_specs=[pl.BlockSpec((tm, d), row)] + [c[1] for c in cspecs],
        scratch_shapes=[
            pltpu.VMEM((tm, d), _BF16),
            pltpu.VMEM((tm, dff), _BF16),
        ],
        compiler_params=_params("arbitrary"),
        name="outproj_swiglu",
    )(h, m, gn, wo, wg, wu, wd, gf, *[w for w, _ in casts])
    return out, cast_out


def kernel(x, conv_w_in, conv_k, conv_w_out, fourier_w_out, mix_norm_g, ffn_norm_g,
           ffn_w_gate, ffn_w_up, ffn_w_down, final_norm_g):
    bsz, seq, d = x.shape
    depth = mix_norm_g.shape[0]
    dff = ffn_w_gate.shape[-1]

    tm_conv = 1024
    tm_ffn = 512
    fchunk = 256
    assert seq % tm_conv == 0 and (bsz * seq) % tm_ffn == 0 and dff % fchunk == 0

    n1, n2, l1, c2, s2, twc, tws, cs, perm = _dft_tables(seq, d // FOURIER_GROUPS)
    l1, cs, perm = l1.astype(_BF16), cs.astype(_BF16), perm.astype(_BF16)
    jb = 2
    kb = 4
    assert d % FOURIER_GROUPS == 0 and seq == n1 * n2
    assert n2 % (V7X_BF16_SUBLANES * jb) == 0 and n1 % V7X_BF16_SUBLANES == 0

    gfin = final_norm_g.reshape(1, d)
    is_conv = lambda i: i % 2 == 0

    def ffn_casts(i):
        wo = (conv_w_out, i // 2) if is_conv(i) else (fourier_w_out, i // 2)
        return [wo, (ffn_w_gate, i), (ffn_w_up, i), (ffn_w_down, i)]

    def next_layer_casts(i):
        if i + 1 >= depth:
            return []
        return ([(conv_w_in, (i + 1) // 2)] if is_conv(i + 1) else []) + ffn_casts(i + 1)

    w_in = conv_w_in[0].astype(_BF16)
    ffn_w = None
    h = x
    for i in range(depth):
        j = i // 2
        gmix = mix_norm_g[i].reshape(1, d)
        if is_conv(i):
            m, cast = _conv_mixer(h, gmix, w_in, conv_k[j], ffn_casts(i) if ffn_w is None else [],
                                  tm=tm_conv, nsub=2)
            ffn_w = cast or ffn_w
        else:
            t1 = _fourier_stage1(h, gmix, l1, n1=n1, n2=n2, jb=jb)
            m = _fourier_stage2(t1, c2, s2, twc, tws, cs, perm, n1=n1, n2=n2, d=d, kb=kb)
        h, cast = _ffn(h.reshape(bsz * seq, d), m.reshape(bsz * seq, d),
                       ffn_norm_g[i].reshape(1, d), *ffn_w, gfin, next_layer_casts(i),
                       tm=tm_ffn, fchunk=fchunk, final=(i == depth - 1))
        h = h.reshape(bsz, seq, d)
        if i + 1 < depth and is_conv(i + 1):
            w_in, *cast = cast
        ffn_w = cast
    return h
```

```python
import functools

import numpy as np
import jax
import jax.numpy as jnp
from jax import lax
from jax.experimental import pallas as pl
from jax.experimental.pallas import tpu as pltpu

RMS_EPS = 1e-5
FOURIER_GROUPS = 8

V7X_LANES = 128
V7X_F32_SUBLANES = 8
V7X_BF16_SUBLANES = 16
V7X_MXU_DIM = 256
V7X_VMEM_LIMIT_BYTES = 56 * 1024 * 1024

_BF16 = jnp.bfloat16
_F32 = jnp.float32


def _dot(a, b):
    return jnp.dot(a, b, preferred_element_type=_F32)


def _rmsnorm(x, g):
    r = lax.rsqrt(jnp.mean(x * x, axis=-1, keepdims=True) + RMS_EPS)
    return x * r * g


def _resident(shape):
    return pl.BlockSpec(shape, lambda *_: (0,) * len(shape), pipeline_mode=pl.Buffered(1))


def _params(*semantics):
    return pltpu.CompilerParams(dimension_semantics=semantics,
                                vmem_limit_bytes=V7X_VMEM_LIMIT_BYTES)


def _cast_specs(stacked, layer, steps, step_of):
    _, rows, cols = stacked.shape
    nblk = steps
    while rows % (nblk * V7X_BF16_SUBLANES):
        nblk //= 2
    rb, every = rows // nblk, steps // nblk
    in_spec = pl.BlockSpec((None, rb, cols), lambda *g: (layer, step_of(*g) // every, 0))
    out_spec = pl.BlockSpec((rb, cols), lambda *g: (step_of(*g) // every, 0))
    return in_spec, out_spec, jax.ShapeDtypeStruct((rows, cols), _BF16)


def _with_casts(body, n_in, n_out, n_cast):
    def kern(*refs):
        ins, rest = refs[:n_in], refs[n_in:]
        cast_in, rest = rest[:n_cast], rest[n_cast:]
        outs, rest = rest[:n_out], rest[n_out:]
        cast_out, scratch = rest[:n_cast], rest[n_cast:]
        for src, dst in zip(cast_in, cast_out):
            dst[...] = src[...].astype(_BF16)
        body(*ins, *outs, *scratch)
    return kern


def _conv_mixer_kernel(h_ref, hprev_ref, hnext_ref, g_ref, win_ref, k_ref, m_ref,
                       xn_ref, *, tm, nsub, d):
    halo = V7X_F32_SUBLANES
    i = pl.program_id(1)
    g = g_ref[...]
    cs = tm // nsub
    xn_ref[0:halo] = _rmsnorm(hnext_ref[...], g).astype(_BF16)
    xn_ref[halo:2 * halo] = _rmsnorm(hprev_ref[...], g).astype(_BF16)
    for q in range(nsub):
        xn_ref[2 * halo + q * cs:2 * halo + (q + 1) * cs] = (
            _rmsnorm(h_ref[q * cs:(q + 1) * cs], g).astype(_BF16))

    starts = [0] + [2 * halo + q * cs for q in range(1, nsub)]
    u = []
    for q in range(nsub):
        x = xn_ref[starts[q]:2 * halo + (q + 1) * cs]
        u.append(_dot(x, win_ref[:, d:2 * d]) * _dot(x, win_ref[:, 2 * d:3 * d]))
    u_next = jnp.where(i == pl.num_programs(1) - 1, 0.0, u[0][0:halo])
    u_prev = jnp.where(i == 0, 0.0, u[0][halo:2 * halo])
    u[0] = u[0][2 * halo:]

    for q in range(nsub):
        before = u_prev if q == 0 else u[q - 1][cs - halo:]
        after = u_next if q == nsub - 1 else u[q + 1][0:halo]
        slab = jnp.concatenate([before, u[q], after], axis=0)
        up = pltpu.roll(slab, 1, 0)[halo:halo + cs]
        dn = pltpu.roll(slab, cs + 2 * halo - 1, 0)[halo:halo + cs]
        conv = k_ref[0:1] * up + k_ref[1:2] * u[q] + k_ref[2:3] * dn
        rows = slice(2 * halo + q * cs, 2 * halo + (q + 1) * cs)
        b = _dot(xn_ref[rows], win_ref[:, 0:d])
        m_ref[q * cs:(q + 1) * cs] = (b * conv).astype(_BF16)


def _conv_mixer(h, g, w_in, k, casts, *, tm, nsub):
    bsz, seq, d = h.shape
    halo = V7X_F32_SUBLANES
    nt = seq // tm
    hb = tm // halo
    last = seq // halo - 1
    cspecs = [_cast_specs(w, l, bsz * nt, lambda b, i: b * nt + i) for w, l in casts]
    body = functools.partial(_conv_mixer_kernel, tm=tm, nsub=nsub, d=d)
    m, *cast_out = pl.pallas_call(
        _with_casts(body, 6, 1, len(casts)),
        out_shape=[jax.ShapeDtypeStruct((bsz, seq, d), _BF16)] + [c[2] for c in cspecs],
        grid=(bsz, nt),
        in_specs=[
            pl.BlockSpec((None, tm, d), lambda b, i: (b, i, 0)),
            pl.BlockSpec((None, halo, d), lambda b, i: (b, jnp.maximum(i * hb - 1, 0), 0)),
            pl.BlockSpec((None, halo, d), lambda b, i: (b, jnp.minimum((i + 1) * hb, last), 0)),
            _resident((1, d)),
            _resident((d, 3 * d)),
            _resident((3, d)),
        ] + [c[0] for c in cspecs],
        out_specs=[pl.BlockSpec((None, tm, d), lambda b, i: (b, i, 0))] + [c[1] for c in cspecs],
        scratch_shapes=[pltpu.VMEM((tm + 2 * halo, d), _BF16)],
        compiler_params=_params("arbitrary", "arbitrary"),
        name="conv_mixer",
    )(h, h, h, g, w_in, k, *[w for w, _ in casts])
    return m, cast_out


def _dft_tables(seq, gd):
    n1 = V7X_MXU_DIM // V7X_F32_SUBLANES
    n2 = seq // n1
    grp = V7X_BF16_SUBLANES
    two_pi = 2.0 * np.pi
    th1 = two_pi * np.outer(np.arange(n1), np.arange(n1)) / n1
    f1 = np.stack([np.cos(th1), -np.sin(th1)], axis=1).reshape(2 * n1, n1) / np.sqrt(n1)
    l1 = np.kron(f1, np.eye(V7X_F32_SUBLANES))
    th2 = two_pi * np.outer(np.arange(n2), np.arange(n2)) / n2
    c2 = np.cos(th2) / np.sqrt(n2)
    s2 = np.sin(th2) / np.sqrt(n2)
    tw = (two_pi * np.outer(np.arange(n1), np.arange(n2)) / seq).reshape(n1, 1, n2)
    ph = two_pi * np.outer(np.arange(gd), np.arange(gd)) / gd
    cs = np.concatenate([np.cos(ph), np.sin(ph)], axis=0) / np.sqrt(gd)
    perm = np.zeros((grp * grp, grp * grp))
    a, b = np.meshgrid(np.arange(grp), np.arange(grp), indexing="ij")
    perm[(b * grp + a).ravel(), (a * grp + b).ravel()] = 1.0
    f = lambda t: jnp.asarray(t, _F32)
    return n1, n2, f(l1), f(c2), f(s2), f(np.cos(tw)), f(np.sin(tw)), f(cs), f(perm)


def _fourier_stage1_kernel(h_ref, g_ref, l1_ref, t_ref, *, n1, jb, d):
    g = g_ref[...]
    l1 = l1_ref[...]
    sub = V7X_F32_SUBLANES
    for j in range(jb):
        halves = []
        for half in range(V7X_BF16_SUBLANES // sub):
            x = h_ref[:, j, half * sub:(half + 1) * sub, :].reshape(n1 * sub, d)
            xn = _rmsnorm(x, g).astype(_BF16)
            halves.append(_dot(l1, xn).reshape(2 * n1, sub, d))
        t = jnp.concatenate(halves, axis=1).astype(_BF16)
        t_ref[:, :, j] = t.reshape(n1, 2, V7X_BF16_SUBLANES, d)


def _fourier_stage1(h, g, l1, *, n1, n2, jb):
    bsz, seq, d = h.shape
    grp = V7X_BF16_SUBLANES
    ng = n2 // grp
    kern = functools.partial(_fourier_stage1_kernel, n1=n1, jb=jb, d=d)
    return pl.pallas_call(
        kern,
        out_shape=jax.ShapeDtypeStruct((bsz, n1, 2, ng, grp, d), _BF16),
        grid=(bsz, ng // jb),
        in_specs=[
            pl.BlockSpec((None, n1, jb, grp, d), lambda b, j: (b, 0, j, 0, 0)),
            _resident((1, d)),
            _resident(l1.shape),
        ],
        out_specs=pl.BlockSpec((None, n1, 2, jb, grp, d), lambda b, j: (b, 0, 0, j, 0, 0)),
        compiler_params=_params("parallel", "parallel"),
        name="fourier_stage1",
    )(h.reshape(bsz, n1, ng, grp, d), g, l1)


def _fourier_stage2_kernel(t_ref, c2_ref, s2_ref, twc_ref, tws_ref, cs_ref, p_ref, y_ref,
                           g_scr, ys_scr, *, kb, n2, d, gd):
    i = pl.program_id(2)
    grp = V7X_BF16_SUBLANES
    c2 = c2_ref[...]
    s2 = s2_ref[...]
    cs = cs_ref[...]
    for kk in range(kb):
        twc = twc_ref[kk]
        tws = tws_ref[kk]
        cth = (c2 * twc - s2 * tws).astype(_BF16)
        sth = s2 * twc + c2 * tws
        g_scr[0:n2, 0:n2] = cth
        g_scr[0:n2, n2:2 * n2] = sth.astype(_BF16)
        g_scr[n2:2 * n2, 0:n2] = (-sth).astype(_BF16)
        g_scr[n2:2 * n2, n2:2 * n2] = cth
        u = _dot(g_scr[...], t_ref[kk]).astype(_BF16)
        for gi in range(d // gd):
            gsl = slice(gi * gd, (gi + 1) * gd)
            lhs = jnp.concatenate([u[0:n2, gsl], u[n2:2 * n2, gsl]], axis=1)
            ys_scr[i * kb + kk, :, gsl] = _dot(lhs, cs).astype(_BF16)

    @pl.when(i == pl.num_programs(2) - 1)
    def _():
        p = p_ref[...]
        for k2h in range(n2 // grp):
            rows = slice(k2h * grp, (k2h + 1) * grp)
            stacked = jnp.concatenate([ys_scr[k1l, rows, :] for k1l in range(grp)], axis=0)
            y_ref[rows] = _dot(p, stacked).astype(_BF16).reshape(grp, grp, d)


def _fourier_stage2(t1, c2, s2, twc, tws, cs, perm, *, n1, n2, d, kb):
    bsz = t1.shape[0]
    gd = d // FOURIER_GROUPS
    grp = V7X_BF16_SUBLANES
    steps = grp // kb
    kern = functools.partial(_fourier_stage2_kernel, kb=kb, n2=n2, d=d, gd=gd)
    return pl.pallas_call(
        kern,
        out_shape=jax.ShapeDtypeStruct((bsz, n2, n1 // grp, grp, d), _BF16),
        grid=(bsz, n1 // grp, steps),
        in_specs=[
            pl.BlockSpec((None, kb, 2 * n2, d), lambda b, c, i: (b, c * steps + i, 0, 0)),
            _resident((n2, n2)),
            _resident((n2, n2)),
            pl.BlockSpec((kb, 1, n2), lambda b, c, i: (c * steps + i, 0, 0)),
            pl.BlockSpec((kb, 1, n2), lambda b, c, i: (c * steps + i, 0, 0)),
            _resident((2 * gd, gd)),
            _resident(perm.shape),
        ],
        out_specs=pl.BlockSpec((None, n2, None, grp, d), lambda b, c, i: (b, 0, c, 0, 0)),
        scratch_shapes=[
            pltpu.VMEM((2 * n2, 2 * n2), _BF16),
            pltpu.VMEM((grp, n2, d), _BF16),
        ],
        compiler_params=_params("parallel", "parallel", "arbitrary"),
        name="fourier_stage2",
    )(t1.reshape(bsz, n1, 2 * n2, d), c2, s2, twc, tws, cs, perm)


def _ffn_kernel(h_ref, m_ref, gn_ref, wo_ref, wg_ref, wu_ref, wd_ref, gf_ref, o_ref,
                hn_scr, act_scr, *, dff, fchunk, nsub, final):
    rs = h_ref.shape[0] // nsub
    for q in range(nsub):
        rows = slice(q * rs, (q + 1) * rs)
        h1 = h_ref[rows] + _dot(m_ref[rows], wo_ref[...])
        o_ref[rows] = h1
        hn_scr[rows] = _rmsnorm(h1, gn_ref[...]).astype(_BF16)
    for q in range(nsub):
        rows = slice(q * rs, (q + 1) * rs)
        for c0 in range(0, dff, fchunk):
            a = _dot(hn_scr[rows], wg_ref[:, c0:c0 + fchunk])
            b = _dot(hn_scr[rows], wu_ref[:, c0:c0 + fchunk])
            act_scr[rows, c0:c0 + fchunk] = (a * jax.nn.sigmoid(a) * b).astype(_BF16)
        out = o_ref[rows] + _dot(act_scr[rows], wd_ref[...])
        if final:
            out = _rmsnorm(out, gf_ref[...])
        o_ref[rows] = out


def _ffn(h, m, gn, wo, wg, wu, wd, gf, casts, *, tm, fchunk, nsub, final):
    n, d = h.shape
    dff = wg.shape[1]
    steps = n // tm
    cspecs = [_cast_specs(w, l, steps, lambda i: i) for w, l in casts]
    body = functools.partial(_ffn_kernel, dff=dff, fchunk=fchunk, nsub=nsub, final=final)
    row = lambda i: (i, 0)
    out, *cast_out = pl.pallas_call(
        _with_casts(body, 8, 1, len(casts)),
        out_shape=[jax.ShapeDtypeStruct((n, d), _F32)] + [c[2] for c in cspecs],
        grid=(steps,),
        in_specs=[
            pl.BlockSpec((tm, d), row),
            pl.BlockSpec((tm, d), row),
            _resident((1, d)),
            _resident((d, d)),
            _resident((d, dff)),
            _resident((d, dff)),
            _resident((dff, d)),
            _resident((1, d)),
        ] + [c[0] for c in cspecs],
        out_specs=[pl.BlockSpec((tm, d), row)] + [c[1] for c in cspecs],
        scratch_shapes=[
            pltpu.VMEM((tm, d), _BF16),
            pltpu.VMEM((tm, dff), _BF16),
        ],
        compiler_params=_params("arbitrary"),
        name="outproj_swiglu",
    )(h, m, gn, wo, wg, wu, wd, gf, *[w for w, _ in casts])
    return out, cast_out


def kernel(x, conv_w_in, conv_k, conv_w_out, fourier_w_out, mix_norm_g, ffn_norm_g,
           ffn_w_gate, ffn_w_up, ffn_w_down, final_norm_g):
    bsz, seq, d = x.shape
    depth = mix_norm_g.shape[0]
    dff = ffn_w_gate.shape[-1]

    tm_conv = 1024
    tm_ffn = 512
    fchunk = 256
    assert seq % tm_conv == 0 and (bsz * seq) % tm_ffn == 0 and dff % fchunk == 0

    n1, n2, l1, c2, s2, twc, tws, cs, perm = _dft_tables(seq, d // FOURIER_GROUPS)
    l1, cs, perm = l1.astype(_BF16), cs.astype(_BF16), perm.astype(_BF16)
    jb = 2
    kb = 4
    assert d % FOURIER_GROUPS == 0 and seq == n1 * n2
    assert n2 % (V7X_BF16_SUBLANES * jb) == 0 and n1 % V7X_BF16_SUBLANES == 0

    gfin = final_norm_g.reshape(1, d)
    is_conv = lambda i: i % 2 == 0

    def ffn_casts(i):
        wo = (conv_w_out, i // 2) if is_conv(i) else (fourier_w_out, i // 2)
        return [wo, (ffn_w_gate, i), (ffn_w_up, i), (ffn_w_down, i)]

    def next_layer_casts(i):
        if i + 1 >= depth:
            return []
        return ([(conv_w_in, (i + 1) // 2)] if is_conv(i + 1) else []) + ffn_casts(i + 1)

    w_in = conv_w_in[0].astype(_BF16)
    ffn_w = None
    h = x
    for i in range(depth):
        j = i // 2
        gmix = mix_norm_g[i].reshape(1, d)
        if is_conv(i):
            m, cast = _conv_mixer(h, gmix, w_in, conv_k[j], ffn_casts(i) if ffn_w is None else [],
                                  tm=tm_conv, nsub=2)
            ffn_w = cast or ffn_w
        else:
            t1 = _fourier_stage1(h, gmix, l1, n1=n1, n2=n2, jb=jb)
            m = _fourier_stage2(t1, c2, s2, twc, tws, cs, perm, n1=n1, n2=n2, d=d, kb=kb)
        h, cast = _ffn(h.reshape(bsz * seq, d), m.reshape(bsz * seq, d),
                       ffn_norm_g[i].reshape(1, d), *ffn_w, gfin, next_layer_casts(i),
                       tm=tm_ffn, fchunk=fchunk, nsub=2, final=(i == depth - 1))
        h = h.reshape(bsz, seq, d)
        if i + 1 < depth and is_conv(i + 1):
            w_in, *cast = cast
        ffn_w = cast
    return h
```

```python
import functools

import numpy as np
import jax
import jax.numpy as jnp
from jax import lax
from jax.experimental import pallas as pl
from jax.experimental.pallas import tpu as pltpu

RMS_EPS = 1e-5
FOURIER_GROUPS = 8

V7X_LANES = 128
V7X_F32_SUBLANES = 8
V7X_BF16_SUBLANES = 16
V7X_MXU_DIM = 256
V7X_VMEM_LIMIT_BYTES = 56 * 1024 * 1024

_BF16 = jnp.bfloat16
_F32 = jnp.float32


def _dot(a, b):
    return jnp.dot(a, b, preferred_element_type=_F32)


def _rmsnorm(x, g):
    r = lax.rsqrt(jnp.mean(x * x, axis=-1, keepdims=True) + RMS_EPS)
    return x * r * g


def _resident(shape):
    return pl.BlockSpec(shape, lambda *_: (0,) * len(shape), pipeline_mode=pl.Buffered(1))


def _params(*semantics):
    return pltpu.CompilerParams(dimension_semantics=semantics,
                                vmem_limit_bytes=V7X_VMEM_LIMIT_BYTES)


def _cast_specs(stacked, layer, steps, step_of):
    _, rows, cols = stacked.shape
    nblk = steps
    while rows % (nblk * V7X_BF16_SUBLANES):
        nblk //= 2
    rb, every = rows // nblk, steps // nblk
    in_spec = pl.BlockSpec((None, rb, cols), lambda *g: (layer, step_of(*g) // every, 0))
    out_spec = pl.BlockSpec((rb, cols), lambda *g: (step_of(*g) // every, 0))
    return in_spec, out_spec, jax.ShapeDtypeStruct((rows, cols), _BF16)


def _with_casts(body, n_in, n_out, n_cast):
    def kern(*refs):
        ins, rest = refs[:n_in], refs[n_in:]
        cast_in, rest = rest[:n_cast], rest[n_cast:]
        outs, rest = rest[:n_out], rest[n_out:]
        cast_out, scratch = rest[:n_cast], rest[n_cast:]
        for src, dst in zip(cast_in, cast_out):
            dst[...] = src[...].astype(_BF16)
        body(*ins, *outs, *scratch)
    return kern


def _conv_mixer_kernel(h_ref, hprev_ref, hnext_ref, g_ref, win_ref, k_ref, m_ref,
                       xn_ref, *, tm, nsub, d):
    halo = V7X_F32_SUBLANES
    i = pl.program_id(1)
    g = g_ref[...]
    cs = tm // nsub
    xn_ref[0:halo] = _rmsnorm(hnext_ref[...], g).astype(_BF16)
    xn_ref[halo:2 * halo] = _rmsnorm(hprev_ref[...], g).astype(_BF16)
    for q in range(nsub):
        xn_ref[2 * halo + q * cs:2 * halo + (q + 1) * cs] = (
            _rmsnorm(h_ref[q * cs:(q + 1) * cs], g).astype(_BF16))

    starts = [0] + [2 * halo + q * cs for q in range(1, nsub)]
    u = []
    for q in range(nsub):
        x = xn_ref[starts[q]:2 * halo + (q + 1) * cs]
        u.append(_dot(x, win_ref[:, d:2 * d]) * _dot(x, win_ref[:, 2 * d:3 * d]))
    u_next = jnp.where(i == pl.num_programs(1) - 1, 0.0, u[0][0:halo])
    u_prev = jnp.where(i == 0, 0.0, u[0][halo:2 * halo])
    u[0] = u[0][2 * halo:]

    for q in range(nsub):
        before = u_prev if q == 0 else u[q - 1][cs - halo:]
        after = u_next if q == nsub - 1 else u[q + 1][0:halo]
        slab = jnp.concatenate([before, u[q], after], axis=0)
        up = pltpu.roll(slab, 1, 0)[halo:halo + cs]
        dn = pltpu.roll(slab, cs + 2 * halo - 1, 0)[halo:halo + cs]
        conv = k_ref[0:1] * up + k_ref[1:2] * u[q] + k_ref[2:3] * dn
        rows = slice(2 * halo + q * cs, 2 * halo + (q + 1) * cs)
        b = _dot(xn_ref[rows], win_ref[:, 0:d])
        m_ref[q * cs:(q + 1) * cs] = (b * conv).astype(_BF16)


def _conv_mixer(h, g, w_in, k, casts, *, tm, nsub):
    bsz, seq, d = h.shape
    halo = V7X_F32_SUBLANES
    nt = seq // tm
    hb = tm // halo
    last = seq // halo - 1
    cspecs = [_cast_specs(w, l, bsz * nt, lambda b, i: b * nt + i) for w, l in casts]
    body = functools.partial(_conv_mixer_kernel, tm=tm, nsub=nsub, d=d)
    m, *cast_out = pl.pallas_call(
        _with_casts(body, 6, 1, len(casts)),
        out_shape=[jax.ShapeDtypeStruct((bsz, seq, d), _BF16)] + [c[2] for c in cspecs],
        grid=(bsz, nt),
        in_specs=[
            pl.BlockSpec((None, tm, d), lambda b, i: (b, i, 0)),
            pl.BlockSpec((None, halo, d), lambda b, i: (b, jnp.maximum(i * hb - 1, 0), 0)),
            pl.BlockSpec((None, halo, d), lambda b, i: (b, jnp.minimum((i + 1) * hb, last), 0)),
            _resident((1, d)),
            _resident((d, 3 * d)),
            _resident((3, d)),
        ] + [c[0] for c in cspecs],
        out_specs=[pl.BlockSpec((None, tm, d), lambda b, i: (b, i, 0))] + [c[1] for c in cspecs],
        scratch_shapes=[pltpu.VMEM((tm + 2 * halo, d), _BF16)],
        compiler_params=_params("arbitrary", "arbitrary"),
        name="conv_mixer",
    )(h, h, h, g, w_in, k, *[w for w, _ in casts])
    return m, cast_out


def _dft_tables(seq, gd, kb):
    n1 = V7X_MXU_DIM // V7X_F32_SUBLANES
    n2 = seq // n1
    n1c = n1 // 2 + kb
    grp = V7X_BF16_SUBLANES
    two_pi = 2.0 * np.pi
    th1 = two_pi * np.outer(np.arange(n1c), np.arange(n1)) / n1
    f1 = np.stack([np.cos(th1), -np.sin(th1)], axis=1).reshape(2 * n1c, n1) / np.sqrt(n1)
    l1 = np.kron(f1, np.eye(V7X_F32_SUBLANES))
    th2 = two_pi * np.outer(np.arange(n2), np.arange(n2)) / n2
    c2 = np.cos(th2) / np.sqrt(n2)
    s2 = np.sin(th2) / np.sqrt(n2)
    tw = (two_pi * np.outer(np.arange(n1c), np.arange(n2)) / seq).reshape(n1c, 1, n2)
    ph = two_pi * np.outer(np.arange(gd), np.arange(gd)) / gd
    cs = np.block([[np.cos(ph), np.cos(ph)], [np.sin(ph), -np.sin(ph)]]) / np.sqrt(gd)
    a, b = np.meshgrid(np.arange(grp), np.arange(grp), indexing="ij")
    perm = np.zeros((2, grp * grp, grp * grp))
    perm[0, (b * grp + a).ravel(), (a * grp + b).ravel()] = 1.0
    perm[1, (b * grp + a).ravel(), (a * grp + np.where(a == 0, b, grp - 1 - b)).ravel()] = 1.0
    f = lambda t: jnp.asarray(t, _F32)
    return n1, n2, n1c, f(l1), f(c2), f(s2), f(np.cos(tw)), f(np.sin(tw)), f(cs), f(perm)


def _fourier_stage1_kernel(h_ref, g_ref, l1_ref, t_ref, *, n1, n1c, jb, d):
    g = g_ref[...]
    l1 = l1_ref[...]
    sub = V7X_F32_SUBLANES
    for j in range(jb):
        halves = []
        for half in range(V7X_BF16_SUBLANES // sub):
            x = h_ref[:, j, half * sub:(half + 1) * sub, :].reshape(n1 * sub, d)
            xn = _rmsnorm(x, g).astype(_BF16)
            halves.append(_dot(l1, xn).reshape(2 * n1c, sub, d))
        t = jnp.concatenate(halves, axis=1).astype(_BF16)
        t_ref[:, :, j] = t.reshape(n1c, 2, V7X_BF16_SUBLANES, d)


def _fourier_stage1(h, g, l1, *, n1, n1c, n2, jb):
    bsz, seq, d = h.shape
    grp = V7X_BF16_SUBLANES
    ng = n2 // grp
    kern = functools.partial(_fourier_stage1_kernel, n1=n1, n1c=n1c, jb=jb, d=d)
    return pl.pallas_call(
        kern,
        out_shape=jax.ShapeDtypeStruct((bsz, n1c, 2, ng, grp, d), _BF16),
        grid=(bsz, ng // jb),
        in_specs=[
            pl.BlockSpec((None, n1, jb, grp, d), lambda b, j: (b, 0, j, 0, 0)),
            _resident((1, d)),
            _resident(l1.shape),
        ],
        out_specs=pl.BlockSpec((None, n1c, 2, jb, grp, d), lambda b, j: (b, 0, 0, j, 0, 0)),
        compiler_params=_params("parallel", "parallel"),
        name="fourier_stage1",
    )(h.reshape(bsz, n1, ng, grp, d), g, l1)


def _fourier_stage2_kernel(t_ref, c2_ref, s2_ref, twc_ref, tws_ref, cs_ref, p_ref, y_ref,
                           ylo_scr, yhi_scr, *, kb, n2, d, gd):
    i = pl.program_id(1)
    grp = V7X_BF16_SUBLANES
    nmain = grp // kb

    def slab(kk):
        twc = twc_ref[kk]
        tws = tws_ref[kk]
        cth = (c2_ref[...] * twc - s2_ref[...] * tws).astype(_BF16)
        sth = s2_ref[...] * twc + c2_ref[...] * tws
        gmat = jnp.concatenate(
            [jnp.concatenate([cth, sth.astype(_BF16)], axis=1),
             jnp.concatenate([(-sth).astype(_BF16), cth], axis=1)], axis=0)
        u = _dot(gmat, t_ref[kk]).astype(_BF16)
        out = []
        for gi in range(d // gd):
            gsl = slice(gi * gd, (gi + 1) * gd)
            lhs = jnp.concatenate([u[0:n2, gsl], u[n2:2 * n2, gsl]], axis=1)
            out.append(_dot(lhs, cs_ref[...]))
        return out

    def interleave(src_scr, perm, mirrored):
        nblk = n2 // grp
        for k2h in range(nblk):
            pieces = []
            for k1l in range(grp):
                blk = nblk - 1 - k2h if (mirrored and k1l > 0) else k2h
                pieces.append(src_scr[k1l, blk * grp:(blk + 1) * grp, :])
            res = _dot(perm, jnp.concatenate(pieces, axis=0))
            y_ref[k2h * grp:(k2h + 1) * grp] = res.astype(_BF16).reshape(grp, grp, d)

    @pl.when(i < nmain)
    def _():
        for kk in range(kb):
            k1 = i * kb + kk
            mirror = (grp - k1) % grp
            for gi, pq in enumerate(slab(kk)):
                gsl = slice(gi * gd, (gi + 1) * gd)
                ylo_scr[k1, :, gsl] = pq[:, 0:gd].astype(_BF16)
                yhi_scr[mirror, :, gsl] = pq[:, gd:2 * gd].astype(_BF16)

    @pl.when(i == nmain - 1)
    def _():
        interleave(ylo_scr, p_ref[0], mirrored=False)

    @pl.when(i == nmain)
    def _():
        for gi, pq in enumerate(slab(0)):
            yhi_scr[0, :, gi * gd:(gi + 1) * gd] = pq[:, 0:gd].astype(_BF16)
        interleave(yhi_scr, p_ref[1], mirrored=True)


def _fourier_stage2(t1, c2, s2, twc, tws, cs, perm, *, n1, n1c, n2, d, kb):
    bsz = t1.shape[0]
    gd = d // FOURIER_GROUPS
    grp = V7X_BF16_SUBLANES
    nmain = grp // kb
    assert n1 == 2 * grp and n1c == (nmain + 1) * kb
    kern = functools.partial(_fourier_stage2_kernel, kb=kb, n2=n2, d=d, gd=gd)
    return pl.pallas_call(
        kern,
        out_shape=jax.ShapeDtypeStruct((bsz, n2, n1 // grp, grp, d), _BF16),
        grid=(bsz, nmain + 1),
        in_specs=[
            pl.BlockSpec((None, kb, 2 * n2, d), lambda b, i: (b, i, 0, 0)),
            _resident((n2, n2)),
            _resident((n2, n2)),
            pl.BlockSpec((kb, 1, n2), lambda b, i: (i, 0, 0)),
            pl.BlockSpec((kb, 1, n2), lambda b, i: (i, 0, 0)),
            _resident(cs.shape),
            _resident(perm.shape),
        ],
        out_specs=pl.BlockSpec((None, n2, None, grp, d), lambda b, i: (b, 0, i // nmain, 0, 0)),
        scratch_shapes=[
            pltpu.VMEM((grp, n2, d), _BF16),
            pltpu.VMEM((grp, n2, d), _BF16),
        ],
        compiler_params=_params("arbitrary", "arbitrary"),
        name="fourier_stage2",
    )(t1.reshape(bsz, n1c, 2 * n2, d), c2, s2, twc, tws, cs, perm)


def _ffn_kernel(h_ref, m_ref, gn_ref, wo_ref, wg_ref, wu_ref, wd_ref, gf_ref, o_ref,
                hn_scr, act_scr, *, dff, fchunk, nsub, final):
    rs = h_ref.shape[0] // nsub
    for q in range(nsub):
        rows = slice(q * rs, (q + 1) * rs)
        h1 = h_ref[rows] + _dot(m_ref[rows], wo_ref[...])
        o_ref[rows] = h1
        hn_scr[rows] = _rmsnorm(h1, gn_ref[...]).astype(_BF16)
    for q in range(nsub):
        rows = slice(q * rs, (q + 1) * rs)
        for c0 in range(0, dff, fchunk):
            a = _dot(hn_scr[rows], wg_ref[:, c0:c0 + fchunk])
            b = _dot(hn_scr[rows], wu_ref[:, c0:c0 + fchunk])
            act_scr[rows, c0:c0 + fchunk] = (a * jax.nn.sigmoid(a) * b).astype(_BF16)
        out = o_ref[rows] + _dot(act_scr[rows], wd_ref[...])
        if final:
            out = _rmsnorm(out, gf_ref[...])
        o_ref[rows] = out


def _ffn(h, m, gn, wo, wg, wu, wd, gf, casts, *, tm, fchunk, nsub, final):
    n, d = h.shape
    dff = wg.shape[1]
    steps = n // tm
    cspecs = [_cast_specs(w, l, steps, lambda i: i) for w, l in casts]
    body = functools.partial(_ffn_kernel, dff=dff, fchunk=fchunk, nsub=nsub, final=final)
    row = lambda i: (i, 0)
    out, *cast_out = pl.pallas_call(
        _with_casts(body, 8, 1, len(casts)),
        out_shape=[jax.ShapeDtypeStruct((n, d), _F32)] + [c[2] for c in cspecs],
        grid=(steps,),
        in_specs=[
            pl.BlockSpec((tm, d), row),
            pl.BlockSpec((tm, d), row),
            _resident((1, d)),
            _resident((d, d)),
            _resident((d, dff)),
            _resident((d, dff)),
            _resident((dff, d)),
            _resident((1, d)),
        ] + [c[0] for c in cspecs],
        out_specs=[pl.BlockSpec((tm, d), row)] + [c[1] for c in cspecs],
        scratch_shapes=[
            pltpu.VMEM((tm, d), _BF16),
            pltpu.VMEM((tm, dff), _BF16),
        ],
        compiler_params=_params("arbitrary"),
        name="outproj_swiglu",
    )(h, m, gn, wo, wg, wu, wd, gf, *[w for w, _ in casts])
    return out, cast_out


def kernel(x, conv_w_in, conv_k, conv_w_out, fourier_w_out, mix_norm_g, ffn_norm_g,
           ffn_w_gate, ffn_w_up, ffn_w_down, final_norm_g):
    bsz, seq, d = x.shape
    depth = mix_norm_g.shape[0]
    dff = ffn_w_gate.shape[-1]

    tm_conv = 1024
    tm_ffn = 512
    fchunk = 256
    assert seq % tm_conv == 0 and (bsz * seq) % tm_ffn == 0 and dff % fchunk == 0

    jb = 2
    kb = 4
    n1, n2, n1c, l1, c2, s2, twc, tws, cs, perm = _dft_tables(seq, d // FOURIER_GROUPS, kb)
    l1, cs, perm = l1.astype(_BF16), cs.astype(_BF16), perm.astype(_BF16)
    assert d % FOURIER_GROUPS == 0 and seq == n1 * n2
    assert n2 % (V7X_BF16_SUBLANES * jb) == 0 and n1 % V7X_BF16_SUBLANES == 0

    gfin = final_norm_g.reshape(1, d)
    is_conv = lambda i: i % 2 == 0

    def ffn_casts(i):
        wo = (conv_w_out, i // 2) if is_conv(i) else (fourier_w_out, i // 2)
        return [wo, (ffn_w_gate, i), (ffn_w_up, i), (ffn_w_down, i)]

    def next_layer_casts(i):
        if i + 1 >= depth:
            return []
        return ([(conv_w_in, (i + 1) // 2)] if is_conv(i + 1) else []) + ffn_casts(i + 1)

    w_in = conv_w_in[0].astype(_BF16)
    ffn_w = None
    h = x
    for i in range(depth):
        j = i // 2
        gmix = mix_norm_g[i].reshape(1, d)
        if is_conv(i):
            m, cast = _conv_mixer(h, gmix, w_in, conv_k[j], ffn_casts(i) if ffn_w is None else [],
                                  tm=tm_conv, nsub=2)
            ffn_w = cast or ffn_w
        else:
            t1 = _fourier_stage1(h, gmix, l1, n1=n1, n1c=n1c, n2=n2, jb=jb)
            m = _fourier_stage2(t1, c2, s2, twc, tws, cs, perm, n1=n1, n1c=n1c, n2=n2, d=d, kb=kb)
        h, cast = _ffn(h.reshape(bsz * seq, d), m.reshape(bsz * seq, d),
                       ffn_norm_g[i].reshape(1, d), *ffn_w, gfin, next_layer_casts(i),
                       tm=tm_ffn, fchunk=fchunk, nsub=2, final=(i == depth - 1))
        h = h.reshape(bsz, seq, d)
        if i + 1 < depth and is_conv(i + 1):
            w_in, *cast = cast
        ffn_w = cast
    return h
```

```python
import functools

import numpy as np
import jax
import jax.numpy as jnp
from jax import lax
from jax.experimental import pallas as pl
from jax.experimental.pallas import tpu as pltpu

RMS_EPS = 1e-5
FOURIER_GROUPS = 8

V7X_LANES = 128
V7X_F32_SUBLANES = 8
V7X_BF16_SUBLANES = 16
V7X_MXU_DIM = 256
V7X_VMEM_LIMIT_BYTES = 56 * 1024 * 1024

_BF16 = jnp.bfloat16
_F32 = jnp.float32


def _dot(a, b):
    return jnp.dot(a, b, preferred_element_type=_F32)


def _rmsnorm(x, g):
    r = lax.rsqrt(jnp.mean(x * x, axis=-1, keepdims=True) + RMS_EPS)
    return x * r * g


def _resident(shape):
    return pl.BlockSpec(shape, lambda *_: (0,) * len(shape), pipeline_mode=pl.Buffered(1))


def _params(*semantics):
    return pltpu.CompilerParams(dimension_semantics=semantics,
                                vmem_limit_bytes=V7X_VMEM_LIMIT_BYTES)


def _cast_specs(stacked, layer, steps, step_of):
    _, rows, cols = stacked.shape
    nblk = steps
    while rows % (nblk * V7X_BF16_SUBLANES):
        nblk //= 2
    rb, every = rows // nblk, steps // nblk
    in_spec = pl.BlockSpec((None, rb, cols), lambda *g: (layer, step_of(*g) // every, 0))
    out_spec = pl.BlockSpec((rb, cols), lambda *g: (step_of(*g) // every, 0))
    return in_spec, out_spec, jax.ShapeDtypeStruct((rows, cols), _BF16)


def _with_casts(body, n_in, n_out, n_cast):
    def kern(*refs):
        ins, rest = refs[:n_in], refs[n_in:]
        cast_in, rest = rest[:n_cast], rest[n_cast:]
        outs, rest = rest[:n_out], rest[n_out:]
        cast_out, scratch = rest[:n_cast], rest[n_cast:]
        for src, dst in zip(cast_in, cast_out):
            dst[...] = src[...].astype(_BF16)
        body(*ins, *outs, *scratch)
    return kern


def _conv_mixer_kernel(h_ref, hprev_ref, hnext_ref, g_ref, win_ref, k_ref, m_ref,
                       xn_ref, *, tm, nsub, d):
    halo = V7X_F32_SUBLANES
    i = pl.program_id(1)
    g = g_ref[...]
    cs = tm // nsub
    xn_ref[0:halo] = _rmsnorm(hnext_ref[...], g).astype(_BF16)
    xn_ref[halo:2 * halo] = _rmsnorm(hprev_ref[...], g).astype(_BF16)
    for q in range(nsub):
        xn_ref[2 * halo + q * cs:2 * halo + (q + 1) * cs] = (
            _rmsnorm(h_ref[q * cs:(q + 1) * cs], g).astype(_BF16))

    starts = [0] + [2 * halo + q * cs for q in range(1, nsub)]
    u = []
    for q in range(nsub):
        x = xn_ref[starts[q]:2 * halo + (q + 1) * cs]
        u.append(_dot(x, win_ref[:, d:2 * d]) * _dot(x, win_ref[:, 2 * d:3 * d]))
    u_next = jnp.where(i == pl.num_programs(1) - 1, 0.0, u[0][0:halo])
    u_prev = jnp.where(i == 0, 0.0, u[0][halo:2 * halo])
    u[0] = u[0][2 * halo:]

    for q in range(nsub):
        before = u_prev if q == 0 else u[q - 1][cs - halo:]
        after = u_next if q == nsub - 1 else u[q + 1][0:halo]
        slab = jnp.concatenate([before, u[q], after], axis=0)
        up = pltpu.roll(slab, 1, 0)[halo:halo + cs]
        dn = pltpu.roll(slab, cs + 2 * halo - 1, 0)[halo:halo + cs]
        conv = k_ref[0:1] * up + k_ref[1:2] * u[q] + k_ref[2:3] * dn
        rows = slice(2 * halo + q * cs, 2 * halo + (q + 1) * cs)
        b = _dot(xn_ref[rows], win_ref[:, 0:d])
        m_ref[q * cs:(q + 1) * cs] = (b * conv).astype(_BF16)


def _conv_mixer(h, g, w_in, k, casts, *, tm, nsub):
    bsz, seq, d = h.shape
    halo = V7X_F32_SUBLANES
    nt = seq // tm
    hb = tm // halo
    last = seq // halo - 1
    cspecs = [_cast_specs(w, l, bsz * nt, lambda b, i: b * nt + i) for w, l in casts]
    body = functools.partial(_conv_mixer_kernel, tm=tm, nsub=nsub, d=d)
    m, *cast_out = pl.pallas_call(
        _with_casts(body, 6, 1, len(casts)),
        out_shape=[jax.ShapeDtypeStruct((bsz, seq, d), _BF16)] + [c[2] for c in cspecs],
        grid=(bsz, nt),
        in_specs=[
            pl.BlockSpec((None, tm, d), lambda b, i: (b, i, 0)),
            pl.BlockSpec((None, halo, d), lambda b, i: (b, jnp.maximum(i * hb - 1, 0), 0)),
            pl.BlockSpec((None, halo, d), lambda b, i: (b, jnp.minimum((i + 1) * hb, last), 0)),
            _resident((1, d)),
            _resident((d, 3 * d)),
            _resident((3, d)),
        ] + [c[0] for c in cspecs],
        out_specs=[pl.BlockSpec((None, tm, d), lambda b, i: (b, i, 0))] + [c[1] for c in cspecs],
        scratch_shapes=[pltpu.VMEM((tm + 2 * halo, d), _BF16)],
        compiler_params=_params("arbitrary", "arbitrary"),
        name="conv_mixer",
    )(h, h, h, g, w_in, k, *[w for w, _ in casts])
    return m, cast_out


def _dft_tables(seq, gd, kb):
    n1 = V7X_MXU_DIM // V7X_F32_SUBLANES
    n2 = seq // n1
    n1c = n1 // 2 + kb
    grp = V7X_BF16_SUBLANES
    two_pi = 2.0 * np.pi
    th1 = two_pi * np.outer(np.arange(n1c), np.arange(n1)) / n1
    f1 = np.stack([np.cos(th1), -np.sin(th1)], axis=1).reshape(2 * n1c, n1) / np.sqrt(n1)
    l1 = np.kron(f1, np.eye(V7X_F32_SUBLANES))
    th2 = two_pi * np.outer(np.arange(n2), np.arange(n2)) / n2
    c2 = np.cos(th2) / np.sqrt(n2)
    s2 = np.sin(th2) / np.sqrt(n2)
    tw = (two_pi * np.outer(np.arange(n1c), np.arange(n2)) / seq).reshape(n1c, 1, n2)
    ph = two_pi * np.outer(np.arange(gd), np.arange(gd)) / gd
    cs = np.block([[np.cos(ph), np.cos(ph)], [np.sin(ph), -np.sin(ph)]]) / np.sqrt(gd)
    a, b = np.meshgrid(np.arange(grp), np.arange(grp), indexing="ij")
    perm = np.zeros((2, grp * grp, grp * grp))
    perm[0, (b * grp + a).ravel(), (a * grp + b).ravel()] = 1.0
    perm[1, (b * grp + a).ravel(), (a * grp + np.where(a == 0, b, grp - 1 - b)).ravel()] = 1.0
    f = lambda t: jnp.asarray(t, _F32)
    return n1, n2, n1c, f(l1), f(c2), f(s2), f(np.cos(tw)), f(np.sin(tw)), f(cs), f(perm)


def _fourier_stage1_kernel(h_ref, g_ref, l1_ref, t_ref, *, n1, n1c, jb, d):
    g = g_ref[...]
    l1 = l1_ref[...]
    sub = V7X_F32_SUBLANES
    for j in range(jb):
        halves = []
        for half in range(V7X_BF16_SUBLANES // sub):
            x = h_ref[:, j, half * sub:(half + 1) * sub, :].reshape(n1 * sub, d)
            xn = _rmsnorm(x, g).astype(_BF16)
            halves.append(_dot(l1, xn).reshape(2 * n1c, sub, d))
        t = jnp.concatenate(halves, axis=1).astype(_BF16)
        t_ref[:, :, j] = t.reshape(n1c, 2, V7X_BF16_SUBLANES, d)


def _fourier_stage1(h, g, l1, *, n1, n1c, n2, jb):
    bsz, seq, d = h.shape
    grp = V7X_BF16_SUBLANES
    ng = n2 // grp
    kern = functools.partial(_fourier_stage1_kernel, n1=n1, n1c=n1c, jb=jb, d=d)
    return pl.pallas_call(
        kern,
        out_shape=jax.ShapeDtypeStruct((bsz, n1c, 2, ng, grp, d), _BF16),
        grid=(bsz, ng // jb),
        in_specs=[
            pl.BlockSpec((None, n1, jb, grp, d), lambda b, j: (b, 0, j, 0, 0)),
            _resident((1, d)),
            _resident(l1.shape),
        ],
        out_specs=pl.BlockSpec((None, n1c, 2, jb, grp, d), lambda b, j: (b, 0, 0, j, 0, 0)),
        compiler_params=_params("parallel", "parallel"),
        name="fourier_stage1",
    )(h.reshape(bsz, n1, ng, grp, d), g, l1)


def _fourier_stage2_kernel(t_ref, c2_ref, s2_ref, twc_ref, tws_ref, cs_ref, p_ref, y_ref,
                           ylo_scr, yhi_scr, *, kb, n2, d, gd):
    i = pl.program_id(1)
    grp = V7X_BF16_SUBLANES
    nmain = grp // kb

    def slab(kk):
        twc = twc_ref[kk]
        tws = tws_ref[kk]
        cth = (c2_ref[...] * twc - s2_ref[...] * tws).astype(_BF16)
        sth = s2_ref[...] * twc + c2_ref[...] * tws
        gmat = jnp.concatenate(
            [jnp.concatenate([cth, sth.astype(_BF16)], axis=1),
             jnp.concatenate([(-sth).astype(_BF16), cth], axis=1)], axis=0)
        u = _dot(gmat, t_ref[kk]).astype(_BF16)
        out = []
        for gi in range(d // gd):
            gsl = slice(gi * gd, (gi + 1) * gd)
            lhs = jnp.concatenate([u[0:n2, gsl], u[n2:2 * n2, gsl]], axis=1)
            out.append(_dot(lhs, cs_ref[...]))
        return out

    def interleave(src_scr, perm, mirrored):
        nblk = n2 // grp
        for k2h in range(nblk):
            pieces = []
            for k1l in range(grp):
                blk = nblk - 1 - k2h if (mirrored and k1l > 0) else k2h
                pieces.append(src_scr[k1l, blk * grp:(blk + 1) * grp, :])
            res = _dot(perm, jnp.concatenate(pieces, axis=0))
            y_ref[k2h * grp:(k2h + 1) * grp] = res.astype(_BF16).reshape(grp, grp, d)

    @pl.when(i < nmain)
    def _():
        for kk in range(kb):
            k1 = i * kb + kk
            mirror = (grp - k1) % grp
            for gi, pq in enumerate(slab(kk)):
                gsl = slice(gi * gd, (gi + 1) * gd)
                ylo_scr[k1, :, gsl] = pq[:, 0:gd].astype(_BF16)
                yhi_scr[mirror, :, gsl] = pq[:, gd:2 * gd].astype(_BF16)

    @pl.when(i == nmain - 1)
    def _():
        interleave(ylo_scr, p_ref[0], mirrored=False)

    @pl.when(i == nmain)
    def _():
        for gi, pq in enumerate(slab(0)):
            yhi_scr[0, :, gi * gd:(gi + 1) * gd] = pq[:, 0:gd].astype(_BF16)
        interleave(yhi_scr, p_ref[1], mirrored=True)


def _fourier_stage2(t1, c2, s2, twc, tws, cs, perm, *, n1, n1c, n2, d, kb):
    bsz = t1.shape[0]
    gd = d // FOURIER_GROUPS
    grp = V7X_BF16_SUBLANES
    nmain = grp // kb
    assert n1 == 2 * grp and n1c == (nmain + 1) * kb
    kern = functools.partial(_fourier_stage2_kernel, kb=kb, n2=n2, d=d, gd=gd)
    return pl.pallas_call(
        kern,
        out_shape=jax.ShapeDtypeStruct((bsz, n2, n1 // grp, grp, d), _BF16),
        grid=(bsz, nmain + 1),
        in_specs=[
            pl.BlockSpec((None, kb, 2 * n2, d), lambda b, i: (b, i, 0, 0)),
            _resident((n2, n2)),
            _resident((n2, n2)),
            pl.BlockSpec((kb, 1, n2), lambda b, i: (i, 0, 0)),
            pl.BlockSpec((kb, 1, n2), lambda b, i: (i, 0, 0)),
            _resident(cs.shape),
            _resident(perm.shape),
        ],
        out_specs=pl.BlockSpec((None, n2, None, grp, d), lambda b, i: (b, 0, i // nmain, 0, 0)),
        scratch_shapes=[
            pltpu.VMEM((grp, n2, d), _BF16),
            pltpu.VMEM((grp, n2, d), _BF16),
        ],
        compiler_params=_params("arbitrary", "arbitrary"),
        name="fourier_stage2",
    )(t1.reshape(bsz, n1c, 2 * n2, d), c2, s2, twc, tws, cs, perm)


def _ffn_kernel(h_ref, m_ref, gn_ref, wo_ref, wg_ref, wu_ref, wd_ref, gf_ref, o_ref,
                hn_scr, act_scr, *, dff, fchunk, nsub, final):
    rs = h_ref.shape[0] // nsub
    for q in range(nsub):
        rows = slice(q * rs, (q + 1) * rs)
        h1 = h_ref[rows] + _dot(m_ref[rows], wo_ref[...])
        o_ref[rows] = h1
        hn_scr[rows] = _rmsnorm(h1, gn_ref[...]).astype(_BF16)
    for q in range(nsub):
        rows = slice(q * rs, (q + 1) * rs)
        for c0 in range(0, dff, fchunk):
            a = _dot(hn_scr[rows], wg_ref[:, c0:c0 + fchunk])
            b = _dot(hn_scr[rows], wu_ref[:, c0:c0 + fchunk])
            act_scr[rows, c0:c0 + fchunk] = (a * jax.nn.sigmoid(a) * b).astype(_BF16)
        out = o_ref[rows] + _dot(act_scr[rows], wd_ref[...])
        if final:
            out = _rmsnorm(out, gf_ref[...])
        o_ref[rows] = out


def _ffn(h, m, gn, wo, wg, wu, wd, gf, *, tm, fchunk, nsub, final):
    n, d = h.shape
    dff = wg.shape[1]
    kern = functools.partial(_ffn_kernel, dff=dff, fchunk=fchunk, nsub=nsub, final=final)
    row = lambda i: (i, 0)
    return pl.pallas_call(
        kern,
        out_shape=jax.ShapeDtypeStruct((n, d), _F32),
        grid=(n // tm,),
        in_specs=[
            pl.BlockSpec((tm, d), row),
            pl.BlockSpec((tm, d), row),
            _resident((1, d)),
            _resident((d, d)),
            _resident((d, dff)),
            _resident((d, dff)),
            _resident((dff, d)),
            _resident((1, d)),
        ],
        out_specs=pl.BlockSpec((tm, d), row),
        scratch_shapes=[
            pltpu.VMEM((tm, d), _BF16),
            pltpu.VMEM((tm, dff), _BF16),
        ],
        compiler_params=_params("parallel"),
        name="outproj_swiglu",
    )(h, m, gn, wo, wg, wu, wd, gf)


def kernel(x, conv_w_in, conv_k, conv_w_out, fourier_w_out, mix_norm_g, ffn_norm_g,
           ffn_w_gate, ffn_w_up, ffn_w_down, final_norm_g):
    bsz, seq, d = x.shape
    depth = mix_norm_g.shape[0]
    dff = ffn_w_gate.shape[-1]

    tm_conv = 1024
    tm_ffn = 1024
    ffn_nsub = 1
    fchunk = 256
    assert seq % tm_conv == 0 and (bsz * seq) % tm_ffn == 0 and dff % fchunk == 0

    jb = 2
    kb = 4
    n1, n2, n1c, l1, c2, s2, twc, tws, cs, perm = _dft_tables(seq, d // FOURIER_GROUPS, kb)
    l1, cs, perm = l1.astype(_BF16), cs.astype(_BF16), perm.astype(_BF16)
    assert d % FOURIER_GROUPS == 0 and seq == n1 * n2
    assert n2 % (V7X_BF16_SUBLANES * jb) == 0 and n1 % V7X_BF16_SUBLANES == 0

    gfin = final_norm_g.reshape(1, d)
    is_conv = lambda i: i % 2 == 0

    def ffn_casts(i):
        wo = (conv_w_out, i // 2) if is_conv(i) else (fourier_w_out, i // 2)
        return [wo, (ffn_w_gate, i), (ffn_w_up, i), (ffn_w_down, i)]

    def mixer_casts(i):
        nxt = ffn_casts(i + 1) if i + 1 < depth else []
        w_in_next = [(conv_w_in, (i + 2) // 2)] if i + 2 < depth else []
        return ffn_casts(i) + nxt + w_in_next

    w_in = conv_w_in[0].astype(_BF16)
    ffn_w = {}
    h = x
    for i in range(depth):
        j = i // 2
        gmix = mix_norm_g[i].reshape(1, d)
        if is_conv(i):
            m, cast = _conv_mixer(h, gmix, w_in, conv_k[j], mixer_casts(i), tm=tm_conv, nsub=2)
            ffn_w[i], cast = cast[:4], cast[4:]
            if i + 1 < depth:
                ffn_w[i + 1], cast = cast[:4], cast[4:]
            if i + 2 < depth:
                w_in, = cast
        else:
            t1 = _fourier_stage1(h, gmix, l1, n1=n1, n1c=n1c, n2=n2, jb=jb)
            m = _fourier_stage2(t1, c2, s2, twc, tws, cs, perm, n1=n1, n1c=n1c, n2=n2, d=d, kb=kb)
        h = _ffn(h.reshape(bsz * seq, d), m.reshape(bsz * seq, d), ffn_norm_g[i].reshape(1, d),
                 *ffn_w[i], gfin, tm=tm_ffn, fchunk=fchunk, nsub=ffn_nsub, final=(i == depth - 1))
        h = h.reshape(bsz, seq, d)
    return h
```

```python
import functools

import numpy as np
import jax
import jax.numpy as jnp
from jax import lax
from jax.experimental import pallas as pl
from jax.experimental.pallas import tpu as pltpu

RMS_EPS = 1e-5
FOURIER_GROUPS = 8

V7X_LANES = 128
V7X_F32_SUBLANES = 8
V7X_BF16_SUBLANES = 16
V7X_MXU_DIM = 256
V7X_VMEM_LIMIT_BYTES = 56 * 1024 * 1024

_BF16 = jnp.bfloat16
_F32 = jnp.float32


def _dot(a, b):
    return jnp.dot(a, b, preferred_element_type=_F32)


def _rmsnorm(x, g):
    r = lax.rsqrt(jnp.mean(x * x, axis=-1, keepdims=True) + RMS_EPS)
    return x * r * g


def _resident(shape):
    return pl.BlockSpec(shape, lambda *_: (0,) * len(shape), pipeline_mode=pl.Buffered(1))


def _params(*semantics):
    return pltpu.CompilerParams(dimension_semantics=semantics,
                                vmem_limit_bytes=V7X_VMEM_LIMIT_BYTES)


def _cast_specs(stacked, layer, steps, step_of):
    _, rows, cols = stacked.shape
    nblk = steps
    while rows % (nblk * V7X_BF16_SUBLANES):
        nblk //= 2
    rb, every = rows // nblk, steps // nblk
    in_spec = pl.BlockSpec((None, rb, cols), lambda *g: (layer, step_of(*g) // every, 0))
    out_spec = pl.BlockSpec((rb, cols), lambda *g: (step_of(*g) // every, 0))
    return in_spec, out_spec, jax.ShapeDtypeStruct((rows, cols), _BF16)


def _with_casts(body, n_in, n_out, n_cast):
    def kern(*refs):
        ins, rest = refs[:n_in], refs[n_in:]
        cast_in, rest = rest[:n_cast], rest[n_cast:]
        outs, rest = rest[:n_out], rest[n_out:]
        cast_out, scratch = rest[:n_cast], rest[n_cast:]
        for src, dst in zip(cast_in, cast_out):
            dst[...] = src[...].astype(_BF16)
        body(*ins, *outs, *scratch)
    return kern


def _conv_mixer_kernel(h_ref, hprev_ref, hnext_ref, g_ref, win_ref, k_ref, m_ref,
                       xn_ref, *, tm, nsub, d):
    halo = V7X_F32_SUBLANES
    i = pl.program_id(1)
    g = g_ref[...]
    cs = tm // nsub
    xn_ref[0:halo] = _rmsnorm(hnext_ref[...], g).astype(_BF16)
    xn_ref[halo:2 * halo] = _rmsnorm(hprev_ref[...], g).astype(_BF16)
    for q in range(nsub):
        xn_ref[2 * halo + q * cs:2 * halo + (q + 1) * cs] = (
            _rmsnorm(h_ref[q * cs:(q + 1) * cs], g).astype(_BF16))

    starts = [0] + [2 * halo + q * cs for q in range(1, nsub)]
    u = []
    for q in range(nsub):
        x = xn_ref[starts[q]:2 * halo + (q + 1) * cs]
        u.append(_dot(x, win_ref[:, d:2 * d]) * _dot(x, win_ref[:, 2 * d:3 * d]))
    u_next = jnp.where(i == pl.num_programs(1) - 1, 0.0, u[0][0:halo])
    u_prev = jnp.where(i == 0, 0.0, u[0][halo:2 * halo])
    u[0] = u[0][2 * halo:]

    for q in range(nsub):
        before = u_prev if q == 0 else u[q - 1][cs - halo:]
        after = u_next if q == nsub - 1 else u[q + 1][0:halo]
        slab = jnp.concatenate([before, u[q], after], axis=0)
        up = pltpu.roll(slab, 1, 0)[halo:halo + cs]
        dn = pltpu.roll(slab, cs + 2 * halo - 1, 0)[halo:halo + cs]
        conv = k_ref[0:1] * up + k_ref[1:2] * u[q] + k_ref[2:3] * dn
        rows = slice(2 * halo + q * cs, 2 * halo + (q + 1) * cs)
        b = _dot(xn_ref[rows], win_ref[:, 0:d])
        m_ref[q * cs:(q + 1) * cs] = (b * conv).astype(_BF16)


def _conv_mixer(h, g, w_in, k, casts, *, tm, nsub):
    bsz, seq, d = h.shape
    halo = V7X_F32_SUBLANES
    nt = seq // tm
    hb = tm // halo
    last = seq // halo - 1
    cspecs = [_cast_specs(w, l, bsz * nt, lambda b, i: b * nt + i) for w, l in casts]
    body = functools.partial(_conv_mixer_kernel, tm=tm, nsub=nsub, d=d)
    m, *cast_out = pl.pallas_call(
        _with_casts(body, 6, 1, len(casts)),
        out_shape=[jax.ShapeDtypeStruct((bsz, seq, d), _BF16)] + [c[2] for c in cspecs],
        grid=(bsz, nt),
        in_specs=[
            pl.BlockSpec((None, tm, d), lambda b, i: (b, i, 0)),
            pl.BlockSpec((None, halo, d), lambda b, i: (b, jnp.maximum(i * hb - 1, 0), 0)),
            pl.BlockSpec((None, halo, d), lambda b, i: (b, jnp.minimum((i + 1) * hb, last), 0)),
            _resident((1, d)),
            _resident((d, 3 * d)),
            _resident((3, d)),
        ] + [c[0] for c in cspecs],
        out_specs=[pl.BlockSpec((None, tm, d), lambda b, i: (b, i, 0))] + [c[1] for c in cspecs],
        scratch_shapes=[pltpu.VMEM((tm + 2 * halo, d), _BF16)],
        compiler_params=_params("arbitrary", "arbitrary"),
        name="conv_mixer",
    )(h, h, h, g, w_in, k, *[w for w, _ in casts])
    return m, cast_out


def _dft_tables(seq, gd, kb):
    n1 = V7X_MXU_DIM // V7X_F32_SUBLANES
    n2 = seq // n1
    n1c = n1 // 2 + kb
    grp = V7X_BF16_SUBLANES
    two_pi = 2.0 * np.pi
    th1 = two_pi * np.outer(np.arange(n1c), np.arange(n1)) / n1
    f1 = np.stack([np.cos(th1), -np.sin(th1)], axis=1).reshape(2 * n1c, n1) / np.sqrt(n1)
    l1 = np.kron(f1, np.eye(V7X_F32_SUBLANES))
    th2 = two_pi * np.outer(np.arange(n2), np.arange(n2)) / n2
    c2 = np.cos(th2) / np.sqrt(n2)
    s2 = np.sin(th2) / np.sqrt(n2)
    tw = (two_pi * np.outer(np.arange(n1c), np.arange(n2)) / seq).reshape(n1c, 1, n2)
    ph = two_pi * np.outer(np.arange(gd), np.arange(gd)) / gd
    cs = np.block([[np.cos(ph), np.cos(ph)], [np.sin(ph), -np.sin(ph)]]) / np.sqrt(gd)
    a, b = np.meshgrid(np.arange(grp), np.arange(grp), indexing="ij")
    perm = np.zeros((2, grp * grp, grp * grp))
    perm[0, (b * grp + a).ravel(), (a * grp + b).ravel()] = 1.0
    perm[1, (b * grp + a).ravel(), (a * grp + np.where(a == 0, b, grp - 1 - b)).ravel()] = 1.0
    f = lambda t: jnp.asarray(t, _F32)
    return n1, n2, n1c, f(l1), f(c2), f(s2), f(np.cos(tw)), f(np.sin(tw)), f(cs), f(perm)


def _fourier_stage1_kernel(h_ref, g_ref, l1_ref, t_ref, *, n1, n1c, jb, d):
    g = g_ref[...]
    l1 = l1_ref[...]
    sub = V7X_F32_SUBLANES
    for j in range(jb):
        halves = []
        for half in range(V7X_BF16_SUBLANES // sub):
            x = h_ref[:, j, half * sub:(half + 1) * sub, :].reshape(n1 * sub, d)
            xn = _rmsnorm(x, g).astype(_BF16)
            halves.append(_dot(l1, xn).reshape(2 * n1c, sub, d))
        t = jnp.concatenate(halves, axis=1).astype(_BF16)
        t_ref[:, :, j] = t.reshape(n1c, 2, V7X_BF16_SUBLANES, d)


def _fourier_stage1(h, g, l1, *, n1, n1c, n2, jb):
    bsz, seq, d = h.shape
    grp = V7X_BF16_SUBLANES
    ng = n2 // grp
    kern = functools.partial(_fourier_stage1_kernel, n1=n1, n1c=n1c, jb=jb, d=d)
    return pl.pallas_call(
        kern,
        out_shape=jax.ShapeDtypeStruct((bsz, n1c, 2, ng, grp, d), _BF16),
        grid=(bsz, ng // jb),
        in_specs=[
            pl.BlockSpec((None, n1, jb, grp, d), lambda b, j: (b, 0, j, 0, 0)),
            _resident((1, d)),
            _resident(l1.shape),
        ],
        out_specs=pl.BlockSpec((None, n1c, 2, jb, grp, d), lambda b, j: (b, 0, 0, j, 0, 0)),
        compiler_params=_params("parallel", "parallel"),
        name="fourier_stage1",
    )(h.reshape(bsz, n1, ng, grp, d), g, l1)


def _fourier_stage2_kernel(t_ref, c2_ref, s2_ref, twc_ref, tws_ref, cs_ref, p_ref, y_ref,
                           ylo_scr, yhi_scr, *, kb, n2, d, gd):
    i = pl.program_id(1)
    grp = V7X_BF16_SUBLANES
    nmain = grp // kb

    def slab(kk):
        twc = twc_ref[kk]
        tws = tws_ref[kk]
        cth = (c2_ref[...] * twc - s2_ref[...] * tws).astype(_BF16)
        sth = s2_ref[...] * twc + c2_ref[...] * tws
        gmat = jnp.concatenate(
            [jnp.concatenate([cth, sth.astype(_BF16)], axis=1),
             jnp.concatenate([(-sth).astype(_BF16), cth], axis=1)], axis=0)
        u = _dot(gmat, t_ref[kk]).astype(_BF16)
        out = []
        for gi in range(d // gd):
            gsl = slice(gi * gd, (gi + 1) * gd)
            lhs = jnp.concatenate([u[0:n2, gsl], u[n2:2 * n2, gsl]], axis=1)
            out.append(_dot(lhs, cs_ref[...]))
        return out

    def interleave(src_scr, perm, mirrored):
        nblk = n2 // grp
        for k2h in range(nblk):
            pieces = []
            for k1l in range(grp):
                blk = nblk - 1 - k2h if (mirrored and k1l > 0) else k2h
                pieces.append(src_scr[k1l, blk * grp:(blk + 1) * grp, :])
            res = _dot(perm, jnp.concatenate(pieces, axis=0))
            y_ref[k2h * grp:(k2h + 1) * grp] = res.astype(_BF16).reshape(grp, grp, d)

    @pl.when(i < nmain)
    def _():
        for kk in range(kb):
            k1 = i * kb + kk
            mirror = (grp - k1) % grp
            for gi, pq in enumerate(slab(kk)):
                gsl = slice(gi * gd, (gi + 1) * gd)
                ylo_scr[k1, :, gsl] = pq[:, 0:gd].astype(_BF16)
                yhi_scr[mirror, :, gsl] = pq[:, gd:2 * gd].astype(_BF16)

    @pl.when(i == nmain - 1)
    def _():
        interleave(ylo_scr, p_ref[0], mirrored=False)

    @pl.when(i == nmain)
    def _():
        for gi, pq in enumerate(slab(0)):
            yhi_scr[0, :, gi * gd:(gi + 1) * gd] = pq[:, 0:gd].astype(_BF16)
        interleave(yhi_scr, p_ref[1], mirrored=True)


def _fourier_stage2(t1, c2, s2, twc, tws, cs, perm, *, n1, n1c, n2, d, kb):
    bsz = t1.shape[0]
    gd = d // FOURIER_GROUPS
    grp = V7X_BF16_SUBLANES
    nmain = grp // kb
    assert n1 == 2 * grp and n1c == (nmain + 1) * kb
    kern = functools.partial(_fourier_stage2_kernel, kb=kb, n2=n2, d=d, gd=gd)
    return pl.pallas_call(
        kern,
        out_shape=jax.ShapeDtypeStruct((bsz, n2, n1 // grp, grp, d), _BF16),
        grid=(bsz, nmain + 1),
        in_specs=[
            pl.BlockSpec((None, kb, 2 * n2, d), lambda b, i: (b, i, 0, 0)),
            _resident((n2, n2)),
            _resident((n2, n2)),
            pl.BlockSpec((kb, 1, n2), lambda b, i: (i, 0, 0)),
            pl.BlockSpec((kb, 1, n2), lambda b, i: (i, 0, 0)),
            _resident(cs.shape),
            _resident(perm.shape),
        ],
        out_specs=pl.BlockSpec((None, n2, None, grp, d), lambda b, i: (b, 0, i // nmain, 0, 0)),
        scratch_shapes=[
            pltpu.VMEM((grp, n2, d), _BF16),
            pltpu.VMEM((grp, n2, d), _BF16),
        ],
        compiler_params=_params("arbitrary", "arbitrary"),
        name="fourier_stage2",
    )(t1.reshape(bsz, n1c, 2 * n2, d), c2, s2, twc, tws, cs, perm)


def _ffn_kernel(h_ref, m_ref, gn_ref, wo_ref, wg_ref, wu_ref, wd_ref, gf_ref, o_ref,
                hn_scr, act_scr, *, dff, fchunk, nsub, final):
    rs = h_ref.shape[0] // nsub
    for q in range(nsub):
        rows = slice(q * rs, (q + 1) * rs)
        h1 = h_ref[rows] + _dot(m_ref[rows], wo_ref[...])
        o_ref[rows] = h1
        hn_scr[rows] = _rmsnorm(h1, gn_ref[...]).astype(_BF16)
    for q in range(nsub):
        rows = slice(q * rs, (q + 1) * rs)
        for c0 in range(0, dff, fchunk):
            a = _dot(hn_scr[rows], wg_ref[:, c0:c0 + fchunk])
            b = _dot(hn_scr[rows], wu_ref[:, c0:c0 + fchunk])
            act_scr[rows, c0:c0 + fchunk] = (a * jax.nn.sigmoid(a) * b).astype(_BF16)
        out = o_ref[rows] + _dot(act_scr[rows], wd_ref[...])
        if final:
            out = _rmsnorm(out, gf_ref[...])
        o_ref[rows] = out


def _ffn(h, m, gn, wo, wg, wu, wd, gf, *, tm, fchunk, nsub, final):
    n, d = h.shape
    dff = wg.shape[1]
    kern = functools.partial(_ffn_kernel, dff=dff, fchunk=fchunk, nsub=nsub, final=final)
    row = lambda i: (i, 0)
    return pl.pallas_call(
        kern,
        out_shape=jax.ShapeDtypeStruct((n, d), _F32),
        grid=(n // tm,),
        in_specs=[
            pl.BlockSpec((tm, d), row),
            pl.BlockSpec((tm, d), row),
            _resident((1, d)),
            _resident((d, d)),
            _resident((d, dff)),
            _resident((d, dff)),
            _resident((dff, d)),
            _resident((1, d)),
        ],
        out_specs=pl.BlockSpec((tm, d), row),
        scratch_shapes=[
            pltpu.VMEM((tm, d), _BF16),
            pltpu.VMEM((tm, dff), _BF16),
        ],
        compiler_params=_params("parallel"),
        name="outproj_swiglu",
    )(h, m, gn, wo, wg, wu, wd, gf)


def kernel(x, conv_w_in, conv_k, conv_w_out, fourier_w_out, mix_norm_g, ffn_norm_g,
           ffn_w_gate, ffn_w_up, ffn_w_down, final_norm_g):
    bsz, seq, d = x.shape
    depth = mix_norm_g.shape[0]
    dff = ffn_w_gate.shape[-1]

    tm_conv = 1024
    tm_ffn = 1024
    ffn_nsub = 2
    fchunk = 256
    assert seq % tm_conv == 0 and (bsz * seq) % tm_ffn == 0 and dff % fchunk == 0

    jb = 2
    kb = 4
    n1, n2, n1c, l1, c2, s2, twc, tws, cs, perm = _dft_tables(seq, d // FOURIER_GROUPS, kb)
    l1, cs, perm = l1.astype(_BF16), cs.astype(_BF16), perm.astype(_BF16)
    assert d % FOURIER_GROUPS == 0 and seq == n1 * n2
    assert n2 % (V7X_BF16_SUBLANES * jb) == 0 and n1 % V7X_BF16_SUBLANES == 0

    gfin = final_norm_g.reshape(1, d)
    is_conv = lambda i: i % 2 == 0

    def ffn_casts(i):
        wo = (conv_w_out, i // 2) if is_conv(i) else (fourier_w_out, i // 2)
        return [wo, (ffn_w_gate, i), (ffn_w_up, i), (ffn_w_down, i)]

    def mixer_casts(i):
        nxt = ffn_casts(i + 1) if i + 1 < depth else []
        w_in_next = [(conv_w_in, (i + 2) // 2)] if i + 2 < depth else []
        return ffn_casts(i) + nxt + w_in_next

    w_in = conv_w_in[0].astype(_BF16)
    ffn_w = {}
    h = x
    for i in range(depth):
        j = i // 2
        gmix = mix_norm_g[i].reshape(1, d)
        if is_conv(i):
            m, cast = _conv_mixer(h, gmix, w_in, conv_k[j], mixer_casts(i), tm=tm_conv, nsub=2)
            ffn_w[i], cast = cast[:4], cast[4:]
            if i + 1 < depth:
                ffn_w[i + 1], cast = cast[:4], cast[4:]
            if i + 2 < depth:
                w_in, = cast
        else:
            t1 = _fourier_stage1(h, gmix, l1, n1=n1, n1c=n1c, n2=n2, jb=jb)
            m = _fourier_stage2(t1, c2, s2, twc, tws, cs, perm, n1=n1, n1c=n1c, n2=n2, d=d, kb=kb)
        h = _ffn(h.reshape(bsz * seq, d), m.reshape(bsz * seq, d), ffn_norm_g[i].reshape(1, d),
                 *ffn_w[i], gfin, tm=tm_ffn, fchunk=fchunk, nsub=ffn_nsub, final=(i == depth - 1))
        h = h.reshape(bsz, seq, d)
    return h
```

```python
import functools

import numpy as np
import jax
import jax.numpy as jnp
from jax import lax
from jax.experimental import pallas as pl
from jax.experimental.pallas import tpu as pltpu

RMS_EPS = 1e-5
FOURIER_GROUPS = 8

V7X_LANES = 128
V7X_F32_SUBLANES = 8
V7X_BF16_SUBLANES = 16
V7X_MXU_DIM = 256
V7X_VMEM_LIMIT_BYTES = 56 * 1024 * 1024

_BF16 = jnp.bfloat16
_F32 = jnp.float32


def _dot(a, b):
    return jnp.dot(a, b, preferred_element_type=_F32)


def _rmsnorm(x, g):
    r = lax.rsqrt(jnp.mean(x * x, axis=-1, keepdims=True) + RMS_EPS)
    return x * r * g


def _resident(shape):
    return pl.BlockSpec(shape, lambda *_: (0,) * len(shape), pipeline_mode=pl.Buffered(1))


def _params(*semantics):
    return pltpu.CompilerParams(dimension_semantics=semantics,
                                vmem_limit_bytes=V7X_VMEM_LIMIT_BYTES)


def _cast_specs(stacked, layer, steps, step_of):
    _, rows, cols = stacked.shape
    nblk = steps
    while rows % (nblk * V7X_BF16_SUBLANES):
        nblk //= 2
    rb, every = rows // nblk, steps // nblk
    in_spec = pl.BlockSpec((None, rb, cols), lambda *g: (layer, step_of(*g) // every, 0))
    out_spec = pl.BlockSpec((rb, cols), lambda *g: (step_of(*g) // every, 0))
    return in_spec, out_spec, jax.ShapeDtypeStruct((rows, cols), _BF16)


def _with_casts(body, n_in, n_out, n_cast):
    def kern(*refs):
        ins, rest = refs[:n_in], refs[n_in:]
        cast_in, rest = rest[:n_cast], rest[n_cast:]
        outs, rest = rest[:n_out], rest[n_out:]
        cast_out, scratch = rest[:n_cast], rest[n_cast:]
        for src, dst in zip(cast_in, cast_out):
            dst[...] = src[...].astype(_BF16)
        body(*ins, *outs, *scratch)
    return kern


def _conv_mixer_kernel(h_ref, hprev_ref, hnext_ref, g_ref, win_ref, k_ref, m_ref,
                       xn_ref, *, tm, nsub, d):
    halo = V7X_F32_SUBLANES
    i = pl.program_id(1)
    g = g_ref[...]
    cs = tm // nsub
    xn_ref[0:halo] = _rmsnorm(hnext_ref[...], g).astype(_BF16)
    xn_ref[halo:2 * halo] = _rmsnorm(hprev_ref[...], g).astype(_BF16)
    for q in range(nsub):
        xn_ref[2 * halo + q * cs:2 * halo + (q + 1) * cs] = (
            _rmsnorm(h_ref[q * cs:(q + 1) * cs], g).astype(_BF16))

    starts = [0] + [2 * halo + q * cs for q in range(1, nsub)]
    u = []
    for q in range(nsub):
        x = xn_ref[starts[q]:2 * halo + (q + 1) * cs]
        u.append(_dot(x, win_ref[:, d:2 * d]) * _dot(x, win_ref[:, 2 * d:3 * d]))
    u_next = jnp.where(i == pl.num_programs(1) - 1, 0.0, u[0][0:halo])
    u_prev = jnp.where(i == 0, 0.0, u[0][halo:2 * halo])
    u[0] = u[0][2 * halo:]

    for q in range(nsub):
        before = u_prev if q == 0 else u[q - 1][cs - halo:]
        after = u_next if q == nsub - 1 else u[q + 1][0:halo]
        slab = jnp.concatenate([before, u[q], after], axis=0)
        up = pltpu.roll(slab, 1, 0)[halo:halo + cs]
        dn = pltpu.roll(slab, cs + 2 * halo - 1, 0)[halo:halo + cs]
        conv = k_ref[0:1] * up + k_ref[1:2] * u[q] + k_ref[2:3] * dn
        rows = slice(2 * halo + q * cs, 2 * halo + (q + 1) * cs)
        b = _dot(xn_ref[rows], win_ref[:, 0:d])
        m_ref[q * cs:(q + 1) * cs] = (b * conv).astype(_BF16)


def _conv_mixer(h, g, w_in, k, casts, *, tm, nsub):
    bsz, seq, d = h.shape
    halo = V7X_F32_SUBLANES
    nt = seq // tm
    hb = tm // halo
    last = seq // halo - 1
    cspecs = [_cast_specs(w, l, bsz * nt, lambda b, i: b * nt + i) for w, l in casts]
    body = functools.partial(_conv_mixer_kernel, tm=tm, nsub=nsub, d=d)
    m, *cast_out = pl.pallas_call(
        _with_casts(body, 6, 1, len(casts)),
        out_shape=[jax.ShapeDtypeStruct((bsz, seq, d), _BF16)] + [c[2] for c in cspecs],
        grid=(bsz, nt),
        in_specs=[
            pl.BlockSpec((None, tm, d), lambda b, i: (b, i, 0)),
            pl.BlockSpec((None, halo, d), lambda b, i: (b, jnp.maximum(i * hb - 1, 0), 0)),
            pl.BlockSpec((None, halo, d), lambda b, i: (b, jnp.minimum((i + 1) * hb, last), 0)),
            _resident((1, d)),
            _resident((d, 3 * d)),
            _resident((3, d)),
        ] + [c[0] for c in cspecs],
        out_specs=[pl.BlockSpec((None, tm, d), lambda b, i: (b, i, 0))] + [c[1] for c in cspecs],
        scratch_shapes=[pltpu.VMEM((tm + 2 * halo, d), _BF16)],
        compiler_params=_params("arbitrary", "arbitrary"),
        name="conv_mixer",
    )(h, h, h, g, w_in, k, *[w for w, _ in casts])
    return m, cast_out


def _dft_tables(seq, gd, kb):
    n1 = V7X_MXU_DIM // V7X_F32_SUBLANES
    n2 = seq // n1
    n1c = n1 // 2 + kb
    grp = V7X_BF16_SUBLANES
    two_pi = 2.0 * np.pi
    th1 = two_pi * np.outer(np.arange(n1c), np.arange(n1)) / n1
    f1 = np.stack([np.cos(th1), -np.sin(th1)], axis=1).reshape(2 * n1c, n1) / np.sqrt(n1)
    l1 = np.kron(f1, np.eye(V7X_F32_SUBLANES))
    th2 = two_pi * np.outer(np.arange(n2), np.arange(n2)) / n2
    c2 = np.cos(th2) / np.sqrt(n2)
    s2 = np.sin(th2) / np.sqrt(n2)
    tw = (two_pi * np.outer(np.arange(n1c), np.arange(n2)) / seq).reshape(n1c, 1, n2)
    ph = two_pi * np.outer(np.arange(gd), np.arange(gd)) / gd
    cs = np.block([[np.cos(ph), np.cos(ph)], [np.sin(ph), -np.sin(ph)]]) / np.sqrt(gd)
    a, b = np.meshgrid(np.arange(grp), np.arange(grp), indexing="ij")
    perm = np.zeros((2, grp * grp, grp * grp))
    perm[0, (b * grp + a).ravel(), (a * grp + b).ravel()] = 1.0
    perm[1, (b * grp + a).ravel(), (a * grp + np.where(a == 0, b, grp - 1 - b)).ravel()] = 1.0
    f = lambda t: jnp.asarray(t, _F32)
    return n1, n2, n1c, f(l1), f(c2), f(s2), f(np.cos(tw)), f(np.sin(tw)), f(cs), f(perm)


def _fourier_stage1_kernel(h_ref, g_ref, l1_ref, t_ref, *, n1, n1c, jb, d):
    g = g_ref[...]
    l1 = l1_ref[...]
    sub = V7X_F32_SUBLANES
    for j in range(jb):
        halves = []
        for half in range(V7X_BF16_SUBLANES // sub):
            x = h_ref[:, j, half * sub:(half + 1) * sub, :].reshape(n1 * sub, d)
            xn = _rmsnorm(x, g).astype(_BF16)
            halves.append(_dot(l1, xn).reshape(2 * n1c, sub, d))
        t = jnp.concatenate(halves, axis=1).astype(_BF16)
        t_ref[:, :, j] = t.reshape(n1c, 2, V7X_BF16_SUBLANES, d)


def _fourier_stage1(h, g, l1, *, n1, n1c, n2, jb):
    bsz, seq, d = h.shape
    grp = V7X_BF16_SUBLANES
    ng = n2 // grp
    kern = functools.partial(_fourier_stage1_kernel, n1=n1, n1c=n1c, jb=jb, d=d)
    return pl.pallas_call(
        kern,
        out_shape=jax.ShapeDtypeStruct((bsz, n1c, 2, ng, grp, d), _BF16),
        grid=(bsz, ng // jb),
        in_specs=[
            pl.BlockSpec((None, n1, jb, grp, d), lambda b, j: (b, 0, j, 0, 0)),
            _resident((1, d)),
            _resident(l1.shape),
        ],
        out_specs=pl.BlockSpec((None, n1c, 2, jb, grp, d), lambda b, j: (b, 0, 0, j, 0, 0)),
        compiler_params=_params("parallel", "parallel"),
        name="fourier_stage1",
    )(h.reshape(bsz, n1, ng, grp, d), g, l1)


def _fourier_stage2_kernel(t_ref, c2_ref, s2_ref, twc_ref, tws_ref, cs_ref, p_ref, y_ref,
                           ylo_scr, yhi_scr, *, kb, n2, d, gd):
    i = pl.program_id(1)
    grp = V7X_BF16_SUBLANES
    nmain = grp // kb

    def slab(kk):
        twc = twc_ref[kk]
        tws = tws_ref[kk]
        cth = (c2_ref[...] * twc - s2_ref[...] * tws).astype(_BF16)
        sth = s2_ref[...] * twc + c2_ref[...] * tws
        gmat = jnp.concatenate(
            [jnp.concatenate([cth, sth.astype(_BF16)], axis=1),
             jnp.concatenate([(-sth).astype(_BF16), cth], axis=1)], axis=0)
        u = _dot(gmat, t_ref[kk]).astype(_BF16)
        out = []
        for gi in range(d // gd):
            gsl = slice(gi * gd, (gi + 1) * gd)
            lhs = jnp.concatenate([u[0:n2, gsl], u[n2:2 * n2, gsl]], axis=1)
            out.append(_dot(lhs, cs_ref[...]))
        return out

    def interleave(src_scr, perm, mirrored):
        nblk = n2 // grp
        for k2h in range(nblk):
            pieces = []
            for k1l in range(grp):
                blk = nblk - 1 - k2h if (mirrored and k1l > 0) else k2h
                pieces.append(src_scr[k1l, blk * grp:(blk + 1) * grp, :])
            res = _dot(perm, jnp.concatenate(pieces, axis=0))
            y_ref[k2h * grp:(k2h + 1) * grp] = res.astype(_BF16).reshape(grp, grp, d)

    @pl.when(i < nmain)
    def _():
        for kk in range(kb):
            k1 = i * kb + kk
            mirror = (grp - k1) % grp
            for gi, pq in enumerate(slab(kk)):
                gsl = slice(gi * gd, (gi + 1) * gd)
                ylo_scr[k1, :, gsl] = pq[:, 0:gd].astype(_BF16)
                yhi_scr[mirror, :, gsl] = pq[:, gd:2 * gd].astype(_BF16)

    @pl.when(i == nmain - 1)
    def _():
        interleave(ylo_scr, p_ref[0], mirrored=False)

    @pl.when(i == nmain)
    def _():
        for gi, pq in enumerate(slab(0)):
            yhi_scr[0, :, gi * gd:(gi + 1) * gd] = pq[:, 0:gd].astype(_BF16)
        interleave(yhi_scr, p_ref[1], mirrored=True)


def _fourier_stage2(t1, c2, s2, twc, tws, cs, perm, *, n1, n1c, n2, d, kb):
    bsz = t1.shape[0]
    gd = d // FOURIER_GROUPS
    grp = V7X_BF16_SUBLANES
    nmain = grp // kb
    assert n1 == 2 * grp and n1c == (nmain + 1) * kb
    kern = functools.partial(_fourier_stage2_kernel, kb=kb, n2=n2, d=d, gd=gd)
    return pl.pallas_call(
        kern,
        out_shape=jax.ShapeDtypeStruct((bsz, n2, n1 // grp, grp, d), _BF16),
        grid=(bsz, nmain + 1),
        in_specs=[
            pl.BlockSpec((None, kb, 2 * n2, d), lambda b, i: (b, i, 0, 0)),
            _resident((n2, n2)),
            _resident((n2, n2)),
            pl.BlockSpec((kb, 1, n2), lambda b, i: (i, 0, 0)),
            pl.BlockSpec((kb, 1, n2), lambda b, i: (i, 0, 0)),
            _resident(cs.shape),
            _resident(perm.shape),
        ],
        out_specs=pl.BlockSpec((None, n2, None, grp, d), lambda b, i: (b, 0, i // nmain, 0, 0)),
        scratch_shapes=[
            pltpu.VMEM((grp, n2, d), _BF16),
            pltpu.VMEM((grp, n2, d), _BF16),
        ],
        compiler_params=_params("arbitrary", "arbitrary"),
        name="fourier_stage2",
    )(t1.reshape(bsz, n1c, 2 * n2, d), c2, s2, twc, tws, cs, perm)


def _ffn_kernel(h_ref, m_ref, gn_ref, wo_ref, wg_ref, wu_ref, wd_ref, gf_ref, o_ref,
                hn_scr, act_scr, *, dff, fchunk, nsub, final):
    rs = h_ref.shape[0] // nsub
    for q in range(nsub):
        rows = slice(q * rs, (q + 1) * rs)
        h1 = h_ref[rows] + _dot(m_ref[rows], wo_ref[...])
        o_ref[rows] = h1
        hn_scr[rows] = _rmsnorm(h1, gn_ref[...]).astype(_BF16)
    for q in range(nsub):
        rows = slice(q * rs, (q + 1) * rs)
        for c0 in range(0, dff, fchunk):
            a = _dot(hn_scr[rows], wg_ref[:, c0:c0 + fchunk])
            b = _dot(hn_scr[rows], wu_ref[:, c0:c0 + fchunk])
            act_scr[rows, c0:c0 + fchunk] = (a * jax.nn.sigmoid(a) * b).astype(_BF16)
        out = o_ref[rows] + _dot(act_scr[rows], wd_ref[...])
        if final:
            out = _rmsnorm(out, gf_ref[...])
        o_ref[rows] = out


def _ffn(h, m, gn, wo, wg, wu, wd, gf, *, tm, fchunk, nsub, final):
    n, d = h.shape
    dff = wg.shape[1]
    kern = functools.partial(_ffn_kernel, dff=dff, fchunk=fchunk, nsub=nsub, final=final)
    row = lambda i: (i, 0)
    return pl.pallas_call(
        kern,
        out_shape=jax.ShapeDtypeStruct((n, d), _F32),
        grid=(n // tm,),
        in_specs=[
            pl.BlockSpec((tm, d), row),
            pl.BlockSpec((tm, d), row),
            _resident((1, d)),
            _resident((d, d)),
            _resident((d, dff)),
            _resident((d, dff)),
            _resident((dff, d)),
            _resident((1, d)),
        ],
        out_specs=pl.BlockSpec((tm, d), row),
        scratch_shapes=[
            pltpu.VMEM((tm, d), _BF16),
            pltpu.VMEM((tm, dff), _BF16),
        ],
        compiler_params=_params("parallel"),
        name="outproj_swiglu",
    )(h, m, gn, wo, wg, wu, wd, gf)


def kernel(x, conv_w_in, conv_k, conv_w_out, fourier_w_out, mix_norm_g, ffn_norm_g,
           ffn_w_gate, ffn_w_up, ffn_w_down, final_norm_g):
    bsz, seq, d = x.shape
    depth = mix_norm_g.shape[0]
    dff = ffn_w_gate.shape[-1]

    tm_conv = 1024
    tm_ffn = 1024
    ffn_nsub = 2
    fchunk = 256
    assert seq % tm_conv == 0 and (bsz * seq) % tm_ffn == 0 and dff % fchunk == 0

    jb = 4
    kb = 4
    n1, n2, n1c, l1, c2, s2, twc, tws, cs, perm = _dft_tables(seq, d // FOURIER_GROUPS, kb)
    l1, cs, perm = l1.astype(_BF16), cs.astype(_BF16), perm.astype(_BF16)
    assert d % FOURIER_GROUPS == 0 and seq == n1 * n2
    assert n2 % (V7X_BF16_SUBLANES * jb) == 0 and n1 % V7X_BF16_SUBLANES == 0

    gfin = final_norm_g.reshape(1, d)
    is_conv = lambda i: i % 2 == 0

    def ffn_casts(i):
        wo = (conv_w_out, i // 2) if is_conv(i) else (fourier_w_out, i // 2)
        return [wo, (ffn_w_gate, i), (ffn_w_up, i), (ffn_w_down, i)]

    def mixer_casts(i):
        nxt = ffn_casts(i + 1) if i + 1 < depth else []
        w_in_next = [(conv_w_in, (i + 2) // 2)] if i + 2 < depth else []
        return ffn_casts(i) + nxt + w_in_next

    w_in = conv_w_in[0].astype(_BF16)
    ffn_w = {}
    h = x
    for i in range(depth):
        j = i // 2
        gmix = mix_norm_g[i].reshape(1, d)
        if is_conv(i):
            m, cast = _conv_mixer(h, gmix, w_in, conv_k[j], mixer_casts(i), tm=tm_conv, nsub=1)
            ffn_w[i], cast = cast[:4], cast[4:]
            if i + 1 < depth:
                ffn_w[i + 1], cast = cast[:4], cast[4:]
            if i + 2 < depth:
                w_in, = cast
        else:
            t1 = _fourier_stage1(h, gmix, l1, n1=n1, n1c=n1c, n2=n2, jb=jb)
            m = _fourier_stage2(t1, c2, s2, twc, tws, cs, perm, n1=n1, n1c=n1c, n2=n2, d=d, kb=kb)
        h = _ffn(h.reshape(bsz * seq, d), m.reshape(bsz * seq, d), ffn_norm_g[i].reshape(1, d),
                 *ffn_w[i], gfin, tm=tm_ffn, fchunk=fchunk, nsub=ffn_nsub, final=(i == depth - 1))
        h = h.reshape(bsz, seq, d)
    return h
```

```python
import functools

import numpy as np
import jax
import jax.numpy as jnp
from jax import lax
from jax.experimental import pallas as pl
from jax.experimental.pallas import tpu as pltpu

RMS_EPS = 1e-5
FOURIER_GROUPS = 8

V7X_LANES = 128
V7X_F32_SUBLANES = 8
V7X_BF16_SUBLANES = 16
V7X_MXU_DIM = 256
V7X_VMEM_LIMIT_BYTES = 56 * 1024 * 1024

_BF16 = jnp.bfloat16
_F32 = jnp.float32


def _dot(a, b):
    return jnp.dot(a, b, preferred_element_type=_F32)


def _rmsnorm(x, g):
    r = lax.rsqrt(jnp.mean(x * x, axis=-1, keepdims=True) + RMS_EPS)
    return x * r * g


def _resident(shape):
    return pl.BlockSpec(shape, lambda *_: (0,) * len(shape), pipeline_mode=pl.Buffered(1))


def _params(*semantics):
    return pltpu.CompilerParams(dimension_semantics=semantics,
                                vmem_limit_bytes=V7X_VMEM_LIMIT_BYTES)


def _cast_specs(stacked, layer, steps, step_of):
    _, rows, cols = stacked.shape
    nblk = steps
    while rows % (nblk * V7X_BF16_SUBLANES):
        nblk //= 2
    rb, every = rows // nblk, steps // nblk
    in_spec = pl.BlockSpec((None, rb, cols), lambda *g: (layer, step_of(*g) // every, 0))
    out_spec = pl.BlockSpec((rb, cols), lambda *g: (step_of(*g) // every, 0))
    return in_spec, out_spec, jax.ShapeDtypeStruct((rows, cols), _BF16)


def _with_casts(body, n_in, n_out, n_cast):
    def kern(*refs):
        ins, rest = refs[:n_in], refs[n_in:]
        cast_in, rest = rest[:n_cast], rest[n_cast:]
        outs, rest = rest[:n_out], rest[n_out:]
        cast_out, scratch = rest[:n_cast], rest[n_cast:]
        for src, dst in zip(cast_in, cast_out):
            dst[...] = src[...].astype(_BF16)
        body(*ins, *outs, *scratch)
    return kern


def _conv_mixer_kernel(h_ref, hprev_ref, hnext_ref, g_ref, win_ref, k_ref, m_ref, *, tm, d):
    halo = V7X_F32_SUBLANES
    i = pl.program_id(1)
    g = g_ref[...]
    xn = jnp.concatenate([_rmsnorm(hnext_ref[...], g), _rmsnorm(hprev_ref[...], g),
                          _rmsnorm(h_ref[...], g)], axis=0).astype(_BF16)
    u = _dot(xn, win_ref[:, d:2 * d]) * _dot(xn, win_ref[:, 2 * d:3 * d])
    u_next = jnp.where(i == pl.num_programs(1) - 1, 0.0, u[0:halo])
    u_prev = jnp.where(i == 0, 0.0, u[halo:2 * halo])
    u_mid = u[2 * halo:]
    slab = jnp.concatenate([u_prev, u_mid, u_next], axis=0)
    up = pltpu.roll(slab, 1, 0)[halo:halo + tm]
    dn = pltpu.roll(slab, tm + 2 * halo - 1, 0)[halo:halo + tm]
    conv = k_ref[0:1] * up + k_ref[1:2] * u_mid + k_ref[2:3] * dn
    b = _dot(xn[2 * halo:], win_ref[:, 0:d])
    m_ref[...] = (b * conv).astype(_BF16)


def _conv_mixer(h, g, w_in, k, casts, *, tm):
    bsz, seq, d = h.shape
    halo = V7X_F32_SUBLANES
    nt = seq // tm
    hb = tm // halo
    last = seq // halo - 1
    cspecs = [_cast_specs(w, l, bsz * nt, lambda b, i: b * nt + i) for w, l in casts]
    body = functools.partial(_conv_mixer_kernel, tm=tm, d=d)
    m, *cast_out = pl.pallas_call(
        _with_casts(body, 6, 1, len(casts)),
        out_shape=[jax.ShapeDtypeStruct((bsz, seq, d), _BF16)] + [c[2] for c in cspecs],
        grid=(bsz, nt),
        in_specs=[
            pl.BlockSpec((None, tm, d), lambda b, i: (b, i, 0)),
            pl.BlockSpec((None, halo, d), lambda b, i: (b, jnp.maximum(i * hb - 1, 0), 0)),
            pl.BlockSpec((None, halo, d), lambda b, i: (b, jnp.minimum((i + 1) * hb, last), 0)),
            _resident((1, d)),
            _resident((d, 3 * d)),
            _resident((3, d)),
        ] + [c[0] for c in cspecs],
        out_specs=[pl.BlockSpec((None, tm, d), lambda b, i: (b, i, 0))] + [c[1] for c in cspecs],
        compiler_params=_params("arbitrary", "arbitrary"),
        name="conv_mixer",
    )(h, h, h, g, w_in, k, *[w for w, _ in casts])
    return m, cast_out


def _dft_tables(seq, gd, kb):
    n1 = V7X_MXU_DIM // V7X_F32_SUBLANES
    n2 = seq // n1
    n1c = n1 // 2 + kb
    grp = V7X_BF16_SUBLANES
    two_pi = 2.0 * np.pi
    th1 = two_pi * np.outer(np.arange(n1c), np.arange(n1)) / n1
    f1 = np.stack([np.cos(th1), -np.sin(th1)], axis=1).reshape(2 * n1c, n1) / np.sqrt(n1)
    l1 = np.kron(f1, np.eye(V7X_F32_SUBLANES))
    th2 = two_pi * np.outer(np.arange(n2), np.arange(n2)) / n2
    c2 = np.cos(th2) / np.sqrt(n2)
    s2 = np.sin(th2) / np.sqrt(n2)
    tw = (two_pi * np.outer(np.arange(n1c), np.arange(n2)) / seq).reshape(n1c, 1, n2)
    ph = two_pi * np.outer(np.arange(gd), np.arange(gd)) / gd
    cs = np.block([[np.cos(ph), np.cos(ph)], [np.sin(ph), -np.sin(ph)]]) / np.sqrt(gd)
    a, b = np.meshgrid(np.arange(grp), np.arange(grp), indexing="ij")
    perm = np.zeros((2, grp * grp, grp * grp))
    perm[0, (b * grp + a).ravel(), (a * grp + b).ravel()] = 1.0
    perm[1, (b * grp + a).ravel(), (a * grp + np.where(a == 0, b, grp - 1 - b)).ravel()] = 1.0
    f = lambda t: jnp.asarray(t, _F32)
    return n1, n2, n1c, f(l1), f(c2), f(s2), f(np.cos(tw)), f(np.sin(tw)), f(cs), f(perm)


def _fourier_stage1_kernel(h_ref, g_ref, l1_ref, t_ref, *, n1, n1c, jb, d):
    g = g_ref[...]
    l1 = l1_ref[...]
    sub = V7X_F32_SUBLANES
    for j in range(jb):
        halves = []
        for half in range(V7X_BF16_SUBLANES // sub):
            x = h_ref[:, j, half * sub:(half + 1) * sub, :].reshape(n1 * sub, d)
            xn = _rmsnorm(x, g).astype(_BF16)
            halves.append(_dot(l1, xn).reshape(2 * n1c, sub, d))
        t = jnp.concatenate(halves, axis=1).astype(_BF16)
        t_ref[:, :, j] = t.reshape(n1c, 2, V7X_BF16_SUBLANES, d)


def _fourier_stage1(h, g, l1, *, n1, n1c, n2, jb):
    bsz, seq, d = h.shape
    grp = V7X_BF16_SUBLANES
    ng = n2 // grp
    kern = functools.partial(_fourier_stage1_kernel, n1=n1, n1c=n1c, jb=jb, d=d)
    return pl.pallas_call(
        kern,
        out_shape=jax.ShapeDtypeStruct((bsz, n1c, 2, ng, grp, d), _BF16),
        grid=(bsz, ng // jb),
        in_specs=[
            pl.BlockSpec((None, n1, jb, grp, d), lambda b, j: (b, 0, j, 0, 0)),
            _resident((1, d)),
            _resident(l1.shape),
        ],
        out_specs=pl.BlockSpec((None, n1c, 2, jb, grp, d), lambda b, j: (b, 0, 0, j, 0, 0)),
        compiler_params=_params("parallel", "parallel"),
        name="fourier_stage1",
    )(h.reshape(bsz, n1, ng, grp, d), g, l1)


def _fourier_stage2_kernel(t_ref, c2_ref, s2_ref, twc_ref, tws_ref, cs_ref, p_ref, y_ref,
                           ylo_scr, yhi_scr, *, kb, n2, d, gd):
    i = pl.program_id(1)
    grp = V7X_BF16_SUBLANES
    nmain = grp // kb

    def slab(kk):
        twc = twc_ref[kk]
        tws = tws_ref[kk]
        cth = (c2_ref[...] * twc - s2_ref[...] * tws).astype(_BF16)
        sth = s2_ref[...] * twc + c2_ref[...] * tws
        gmat = jnp.concatenate(
            [jnp.concatenate([cth, sth.astype(_BF16)], axis=1),
             jnp.concatenate([(-sth).astype(_BF16), cth], axis=1)], axis=0)
        u = _dot(gmat, t_ref[kk]).astype(_BF16)
        out = []
        for gi in range(d // gd):
            gsl = slice(gi * gd, (gi + 1) * gd)
            lhs = jnp.concatenate([u[0:n2, gsl], u[n2:2 * n2, gsl]], axis=1)
            out.append(_dot(lhs, cs_ref[...]))
        return out

    def interleave(src_scr, perm, mirrored):
        nblk = n2 // grp
        for k2h in range(nblk):
            pieces = []
            for k1l in range(grp):
                blk = nblk - 1 - k2h if (mirrored and k1l > 0) else k2h
                pieces.append(src_scr[k1l, blk * grp:(blk + 1) * grp, :])
            res = _dot(perm, jnp.concatenate(pieces, axis=0))
            y_ref[k2h * grp:(k2h + 1) * grp] = res.astype(_BF16).reshape(grp, grp, d)

    @pl.when(i < nmain)
    def _():
        for kk in range(kb):
            k1 = i * kb + kk
            mirror = (grp - k1) % grp
            for gi, pq in enumerate(slab(kk)):
                gsl = slice(gi * gd, (gi + 1) * gd)
                ylo_scr[k1, :, gsl] = pq[:, 0:gd].astype(_BF16)
                yhi_scr[mirror, :, gsl] = pq[:, gd:2 * gd].astype(_BF16)

    @pl.when(i == nmain - 1)
    def _():
        interleave(ylo_scr, p_ref[0], mirrored=False)

    @pl.when(i == nmain)
    def _():
        for gi, pq in enumerate(slab(0)):
            yhi_scr[0, :, gi * gd:(gi + 1) * gd] = pq[:, 0:gd].astype(_BF16)
        interleave(yhi_scr, p_ref[1], mirrored=True)


def _fourier_stage2(t1, c2, s2, twc, tws, cs, perm, *, n1, n1c, n2, d, kb):
    bsz = t1.shape[0]
    gd = d // FOURIER_GROUPS
    grp = V7X_BF16_SUBLANES
    nmain = grp // kb
    assert n1 == 2 * grp and n1c == (nmain + 1) * kb
    kern = functools.partial(_fourier_stage2_kernel, kb=kb, n2=n2, d=d, gd=gd)
    return pl.pallas_call(
        kern,
        out_shape=jax.ShapeDtypeStruct((bsz, n2, n1 // grp, grp, d), _BF16),
        grid=(bsz, nmain + 1),
        in_specs=[
            pl.BlockSpec((None, kb, 2 * n2, d), lambda b, i: (b, i, 0, 0)),
            _resident((n2, n2)),
            _resident((n2, n2)),
            pl.BlockSpec((kb, 1, n2), lambda b, i: (i, 0, 0)),
            pl.BlockSpec((kb, 1, n2), lambda b, i: (i, 0, 0)),
            _resident(cs.shape),
            _resident(perm.shape),
        ],
        out_specs=pl.BlockSpec((None, n2, None, grp, d), lambda b, i: (b, 0, i // nmain, 0, 0)),
        scratch_shapes=[
            pltpu.VMEM((grp, n2, d), _BF16),
            pltpu.VMEM((grp, n2, d), _BF16),
        ],
        compiler_params=_params("arbitrary", "arbitrary"),
        name="fourier_stage2",
    )(t1.reshape(bsz, n1c, 2 * n2, d), c2, s2, twc, tws, cs, perm)


def _ffn_kernel(h_ref, m_ref, gn_ref, wo_ref, wg_ref, wu_ref, wd_ref, gf_ref, o_ref,
                hn_scr, act_scr, *, dff, fchunk, nsub, final):
    rs = h_ref.shape[0] // nsub
    for q in range(nsub):
        rows = slice(q * rs, (q + 1) * rs)
        h1 = h_ref[rows] + _dot(m_ref[rows], wo_ref[...])
        o_ref[rows] = h1
        hn_scr[rows] = _rmsnorm(h1, gn_ref[...]).astype(_BF16)
    for q in range(nsub):
        rows = slice(q * rs, (q + 1) * rs)
        for c0 in range(0, dff, fchunk):
            a = _dot(hn_scr[rows], wg_ref[:, c0:c0 + fchunk])
            b = _dot(hn_scr[rows], wu_ref[:, c0:c0 + fchunk])
            act_scr[rows, c0:c0 + fchunk] = (a * jax.nn.sigmoid(a) * b).astype(_BF16)
        out = o_ref[rows] + _dot(act_scr[rows], wd_ref[...])
        if final:
            out = _rmsnorm(out, gf_ref[...])
        o_ref[rows] = out


def _ffn(h, m, gn, wo, wg, wu, wd, gf, *, tm, fchunk, nsub, final):
    n, d = h.shape
    dff = wg.shape[1]
    kern = functools.partial(_ffn_kernel, dff=dff, fchunk=fchunk, nsub=nsub, final=final)
    row = lambda i: (i, 0)
    return pl.pallas_call(
        kern,
        out_shape=jax.ShapeDtypeStruct((n, d), _F32),
        grid=(n // tm,),
        in_specs=[
            pl.BlockSpec((tm, d), row),
            pl.BlockSpec((tm, d), row),
            _resident((1, d)),
            _resident((d, d)),
            _resident((d, dff)),
            _resident((d, dff)),
            _resident((dff, d)),
            _resident((1, d)),
        ],
        out_specs=pl.BlockSpec((tm, d), row),
        scratch_shapes=[
            pltpu.VMEM((tm, d), _BF16),
            pltpu.VMEM((tm, dff), _BF16),
        ],
        compiler_params=_params("parallel"),
        name="outproj_swiglu",
    )(h, m, gn, wo, wg, wu, wd, gf)


def kernel(x, conv_w_in, conv_k, conv_w_out, fourier_w_out, mix_norm_g, ffn_norm_g,
           ffn_w_gate, ffn_w_up, ffn_w_down, final_norm_g):
    bsz, seq, d = x.shape
    depth = mix_norm_g.shape[0]
    dff = ffn_w_gate.shape[-1]

    tm_conv = 1024
    tm_ffn = 1024
    ffn_nsub = 2
    fchunk = 256
    assert seq % tm_conv == 0 and (bsz * seq) % tm_ffn == 0 and dff % fchunk == 0

    jb = 4
    kb = 4
    n1, n2, n1c, l1, c2, s2, twc, tws, cs, perm = _dft_tables(seq, d // FOURIER_GROUPS, kb)
    l1, cs, perm = l1.astype(_BF16), cs.astype(_BF16), perm.astype(_BF16)
    assert d % FOURIER_GROUPS == 0 and seq == n1 * n2
    assert n2 % (V7X_BF16_SUBLANES * jb) == 0 and n1 % V7X_BF16_SUBLANES == 0

    gfin = final_norm_g.reshape(1, d)
    is_conv = lambda i: i % 2 == 0

    def ffn_casts(i):
        wo = (conv_w_out, i // 2) if is_conv(i) else (fourier_w_out, i // 2)
        return [wo, (ffn_w_gate, i), (ffn_w_up, i), (ffn_w_down, i)]

    def mixer_casts(i):
        nxt = ffn_casts(i + 1) if i + 1 < depth else []
        w_in_next = [(conv_w_in, (i + 2) // 2)] if i + 2 < depth else []
        return ffn_casts(i) + nxt + w_in_next

    w_in = conv_w_in[0].astype(_BF16)
    ffn_w = {}
    h = x
    for i in range(depth):
        j = i // 2
        gmix = mix_norm_g[i].reshape(1, d)
        if is_conv(i):
            m, cast = _conv_mixer(h, gmix, w_in, conv_k[j], mixer_casts(i), tm=tm_conv)
            ffn_w[i], cast = cast[:4], cast[4:]
            if i + 1 < depth:
                ffn_w[i + 1], cast = cast[:4], cast[4:]
            if i + 2 < depth:
                w_in, = cast
        else:
            t1 = _fourier_stage1(h, gmix, l1, n1=n1, n1c=n1c, n2=n2, jb=jb)
            m = _fourier_stage2(t1, c2, s2, twc, tws, cs, perm, n1=n1, n1c=n1c, n2=n2, d=d, kb=kb)
        h = _ffn(h.reshape(bsz * seq, d), m.reshape(bsz * seq, d), ffn_norm_g[i].reshape(1, d),
                 *ffn_w[i], gfin, tm=tm_ffn, fchunk=fchunk, nsub=ffn_nsub, final=(i == depth - 1))
        h = h.reshape(bsz, seq, d)
    return h
```

```python
import functools

import numpy as np
import jax
import jax.numpy as jnp
from jax import lax
from jax.experimental import pallas as pl
from jax.experimental.pallas import tpu as pltpu

RMS_EPS = 1e-5
FOURIER_GROUPS = 8

V7X_LANES = 128
V7X_F32_SUBLANES = 8
V7X_BF16_SUBLANES = 16
V7X_MXU_DIM = 256
V7X_VMEM_LIMIT_BYTES = 56 * 1024 * 1024

_BF16 = jnp.bfloat16
_F32 = jnp.float32


def _dot(a, b):
    return jnp.dot(a, b, preferred_element_type=_F32)


def _rmsnorm(x, g):
    r = lax.rsqrt(jnp.mean(x * x, axis=-1, keepdims=True) + RMS_EPS)
    return x * r * g


def _resident(shape):
    return pl.BlockSpec(shape, lambda *_: (0,) * len(shape), pipeline_mode=pl.Buffered(1))


def _params(*semantics):
    return pltpu.CompilerParams(dimension_semantics=semantics,
                                vmem_limit_bytes=V7X_VMEM_LIMIT_BYTES)


def _cast_specs(stacked, layer, steps, step_of):
    _, rows, cols = stacked.shape
    nblk = steps
    while rows % (nblk * V7X_BF16_SUBLANES):
        nblk //= 2
    rb, every = rows // nblk, steps // nblk
    in_spec = pl.BlockSpec((None, rb, cols), lambda *g: (layer, step_of(*g) // every, 0))
    out_spec = pl.BlockSpec((rb, cols), lambda *g: (step_of(*g) // every, 0))
    return in_spec, out_spec, jax.ShapeDtypeStruct((rows, cols), _BF16)


def _with_casts(body, n_in, n_out, n_cast):
    def kern(*refs):
        ins, rest = refs[:n_in], refs[n_in:]
        cast_in, rest = rest[:n_cast], rest[n_cast:]
        outs, rest = rest[:n_out], rest[n_out:]
        cast_out, scratch = rest[:n_cast], rest[n_cast:]
        for src, dst in zip(cast_in, cast_out):
            dst[...] = src[...].astype(_BF16)
        body(*ins, *outs, *scratch)
    return kern


def _conv_mixer_kernel(h_ref, hprev_ref, hnext_ref, g_ref, win_ref, k_ref, m_ref, *, tm, d):
    halo = V7X_F32_SUBLANES
    i = pl.program_id(1)
    g = g_ref[...]
    xn = jnp.concatenate([_rmsnorm(hnext_ref[...], g), _rmsnorm(hprev_ref[...], g),
                          _rmsnorm(h_ref[...], g)], axis=0).astype(_BF16)
    u = _dot(xn, win_ref[:, d:2 * d]) * _dot(xn, win_ref[:, 2 * d:3 * d])
    u_next = jnp.where(i == pl.num_programs(1) - 1, 0.0, u[0:halo])
    u_prev = jnp.where(i == 0, 0.0, u[halo:2 * halo])
    u_mid = u[2 * halo:]
    slab = jnp.concatenate([u_prev, u_mid, u_next], axis=0)
    up = pltpu.roll(slab, 1, 0)[halo:halo + tm]
    dn = pltpu.roll(slab, tm + 2 * halo - 1, 0)[halo:halo + tm]
    conv = k_ref[0:1] * up + k_ref[1:2] * u_mid + k_ref[2:3] * dn
    b = _dot(xn[2 * halo:], win_ref[:, 0:d])
    m_ref[...] = (b * conv).astype(_BF16)


def _conv_mixer(h, g, w_in, k, casts, *, tm):
    bsz, seq, d = h.shape
    halo = V7X_F32_SUBLANES
    nt = seq // tm
    hb = tm // halo
    last = seq // halo - 1
    cspecs = [_cast_specs(w, l, bsz * nt, lambda b, i: b * nt + i) for w, l in casts]
    body = functools.partial(_conv_mixer_kernel, tm=tm, d=d)
    m, *cast_out = pl.pallas_call(
        _with_casts(body, 6, 1, len(casts)),
        out_shape=[jax.ShapeDtypeStruct((bsz, seq, d), _BF16)] + [c[2] for c in cspecs],
        grid=(bsz, nt),
        in_specs=[
            pl.BlockSpec((None, tm, d), lambda b, i: (b, i, 0)),
            pl.BlockSpec((None, halo, d), lambda b, i: (b, jnp.maximum(i * hb - 1, 0), 0)),
            pl.BlockSpec((None, halo, d), lambda b, i: (b, jnp.minimum((i + 1) * hb, last), 0)),
            _resident((1, d)),
            _resident((d, 3 * d)),
            _resident((3, d)),
        ] + [c[0] for c in cspecs],
        out_specs=[pl.BlockSpec((None, tm, d), lambda b, i: (b, i, 0))] + [c[1] for c in cspecs],
        compiler_params=_params("arbitrary", "arbitrary"),
        name="conv_mixer",
    )(h, h, h, g, w_in, k, *[w for w, _ in casts])
    return m, cast_out


def _dft_tables(seq, gd, kb):
    grp = V7X_BF16_SUBLANES
    n1 = 2 * V7X_MXU_DIM // grp
    n2 = seq // n1
    n1c = n1 // 2 + kb
    two_pi = 2.0 * np.pi
    th1 = two_pi * np.outer(np.arange(n1c), np.arange(n1)) / n1
    f1 = np.stack([np.cos(th1), -np.sin(th1)], axis=1).reshape(2 * n1c, n1) / np.sqrt(n1)
    l1 = np.kron(f1, np.eye(grp))
    th2 = two_pi * np.outer(np.arange(n2), np.arange(n2)) / n2
    c2 = np.cos(th2) / np.sqrt(n2)
    s2 = np.sin(th2) / np.sqrt(n2)
    tw = (two_pi * np.outer(np.arange(n1c), np.arange(n2)) / seq).reshape(n1c, 1, n2)
    ph = two_pi * np.outer(np.arange(gd), np.arange(gd)) / gd
    cs = np.block([[np.cos(ph), np.cos(ph)], [np.sin(ph), -np.sin(ph)]]) / np.sqrt(gd)
    a, b = np.meshgrid(np.arange(grp), np.arange(grp), indexing="ij")
    perm = np.zeros((2, grp * grp, grp * grp))
    perm[0, (b * grp + a).ravel(), (a * grp + b).ravel()] = 1.0
    perm[1, (b * grp + a).ravel(), (a * grp + np.where(a == 0, b, grp - 1 - b)).ravel()] = 1.0
    f = lambda t: jnp.asarray(t, _F32)
    return n1, n2, n1c, f(l1), f(c2), f(s2), f(np.cos(tw)), f(np.sin(tw)), f(cs), f(perm)


def _fourier_stage1_kernel(x_ref, l1_ref, t_ref, *, n1, n1c, jb, d):
    grp = V7X_BF16_SUBLANES
    for j in range(jb):
        t = _dot(l1_ref[...], x_ref[:, j].reshape(n1 * grp, d))
        t_ref[:, :, j] = t.astype(_BF16).reshape(n1c, 2, grp, d)


def _fourier_stage1(xn, l1, *, n1, n1c, n2, jb):
    bsz, seq, d = xn.shape
    grp = V7X_BF16_SUBLANES
    ng = n2 // grp
    kern = functools.partial(_fourier_stage1_kernel, n1=n1, n1c=n1c, jb=jb, d=d)
    return pl.pallas_call(
        kern,
        out_shape=jax.ShapeDtypeStruct((bsz, n1c, 2, ng, grp, d), _BF16),
        grid=(bsz, ng // jb),
        in_specs=[
            pl.BlockSpec((None, n1, jb, grp, d), lambda b, j: (b, 0, j, 0, 0)),
            _resident(l1.shape),
        ],
        out_specs=pl.BlockSpec((None, n1c, 2, jb, grp, d), lambda b, j: (b, 0, 0, j, 0, 0)),
        compiler_params=_params("parallel", "parallel"),
        name="fourier_stage1",
    )(xn.reshape(bsz, n1, ng, grp, d), l1)


def _fourier_stage2_kernel(t_ref, c2_ref, s2_ref, twc_ref, tws_ref, cs_ref, p_ref, y_ref,
                           ylo_scr, yhi_scr, *, kb, n2, d, gd):
    i = pl.program_id(1)
    grp = V7X_BF16_SUBLANES
    nmain = grp // kb

    def slab(kk):
        twc = twc_ref[kk]
        tws = tws_ref[kk]
        cth = (c2_ref[...] * twc - s2_ref[...] * tws).astype(_BF16)
        sth = s2_ref[...] * twc + c2_ref[...] * tws
        gmat = jnp.concatenate(
            [jnp.concatenate([cth, sth.astype(_BF16)], axis=1),
             jnp.concatenate([(-sth).astype(_BF16), cth], axis=1)], axis=0)
        u = _dot(gmat, t_ref[kk]).astype(_BF16)
        out = []
        for gi in range(d // gd):
            gsl = slice(gi * gd, (gi + 1) * gd)
            lhs = jnp.concatenate([u[0:n2, gsl], u[n2:2 * n2, gsl]], axis=1)
            out.append(_dot(lhs, cs_ref[...]))
        return out

    def interleave(src_scr, perm, mirrored):
        nblk = n2 // grp
        for k2h in range(nblk):
            pieces = []
            for k1l in range(grp):
                blk = nblk - 1 - k2h if (mirrored and k1l > 0) else k2h
                pieces.append(src_scr[k1l, blk * grp:(blk + 1) * grp, :])
            res = _dot(perm, jnp.concatenate(pieces, axis=0))
            y_ref[k2h * grp:(k2h + 1) * grp] = res.astype(_BF16).reshape(grp, grp, d)

    @pl.when(i < nmain)
    def _():
        for kk in range(kb):
            k1 = i * kb + kk
            mirror = (grp - k1) % grp
            for gi, pq in enumerate(slab(kk)):
                gsl = slice(gi * gd, (gi + 1) * gd)
                ylo_scr[k1, :, gsl] = pq[:, 0:gd].astype(_BF16)
                yhi_scr[mirror, :, gsl] = pq[:, gd:2 * gd].astype(_BF16)

    @pl.when(i == nmain - 1)
    def _():
        interleave(ylo_scr, p_ref[0], mirrored=False)

    @pl.when(i == nmain)
    def _():
        for gi, pq in enumerate(slab(0)):
            yhi_scr[0, :, gi * gd:(gi + 1) * gd] = pq[:, 0:gd].astype(_BF16)
        interleave(yhi_scr, p_ref[1], mirrored=True)


def _fourier_stage2(t1, c2, s2, twc, tws, cs, perm, *, n1, n1c, n2, d, kb):
    bsz = t1.shape[0]
    gd = d // FOURIER_GROUPS
    grp = V7X_BF16_SUBLANES
    nmain = grp // kb
    assert n1 == 2 * grp and n1c == (nmain + 1) * kb
    kern = functools.partial(_fourier_stage2_kernel, kb=kb, n2=n2, d=d, gd=gd)
    return pl.pallas_call(
        kern,
        out_shape=jax.ShapeDtypeStruct((bsz, n2, n1 // grp, grp, d), _BF16),
        grid=(bsz, nmain + 1),
        in_specs=[
            pl.BlockSpec((None, kb, 2 * n2, d), lambda b, i: (b, i, 0, 0)),
            _resident((n2, n2)),
            _resident((n2, n2)),
            pl.BlockSpec((kb, 1, n2), lambda b, i: (i, 0, 0)),
            pl.BlockSpec((kb, 1, n2), lambda b, i: (i, 0, 0)),
            _resident(cs.shape),
            _resident(perm.shape),
        ],
        out_specs=pl.BlockSpec((None, n2, None, grp, d), lambda b, i: (b, 0, i // nmain, 0, 0)),
        scratch_shapes=[
            pltpu.VMEM((grp, n2, d), _BF16),
            pltpu.VMEM((grp, n2, d), _BF16),
        ],
        compiler_params=_params("arbitrary", "arbitrary"),
        name="fourier_stage2",
    )(t1.reshape(bsz, n1c, 2 * n2, d), c2, s2, twc, tws, cs, perm)


def _ffn_kernel(h_ref, m_ref, gn_ref, wo_ref, wg_ref, wu_ref, wd_ref, ge_ref, *rest,
                dff, fchunk, nsub, epilogue):
    if epilogue == "emit_norm":
        o_ref, xn_ref, hn_scr, act_scr = rest
    else:
        o_ref, hn_scr, act_scr = rest
    rs = h_ref.shape[0] // nsub
    for q in range(nsub):
        rows = slice(q * rs, (q + 1) * rs)
        h1 = h_ref[rows] + _dot(m_ref[rows], wo_ref[...])
        o_ref[rows] = h1
        hn_scr[rows] = _rmsnorm(h1, gn_ref[...]).astype(_BF16)
    for q in range(nsub):
        rows = slice(q * rs, (q + 1) * rs)
        for c0 in range(0, dff, fchunk):
            a = _dot(hn_scr[rows], wg_ref[:, c0:c0 + fchunk])
            b = _dot(hn_scr[rows], wu_ref[:, c0:c0 + fchunk])
            act_scr[rows, c0:c0 + fchunk] = (a * jax.nn.sigmoid(a) * b).astype(_BF16)
        parts = 2 if (q == nsub - 1 and epilogue != "residual") else 1
        for p in range(parts):
            part = slice(q * rs + p * rs // parts, q * rs + (p + 1) * rs // parts)
            out = o_ref[part] + _dot(act_scr[part], wd_ref[...])
            if epilogue == "final_norm":
                out = _rmsnorm(out, ge_ref[...])
            o_ref[part] = out
            if epilogue == "emit_norm":
                xn_ref[part] = _rmsnorm(out, ge_ref[...]).astype(_BF16)


def _ffn(h, m, gn, wo, wg, wu, wd, ge, *, tm, fchunk, nsub, epilogue):
    n, d = h.shape
    dff = wg.shape[1]
    emit = epilogue == "emit_norm"
    kern = functools.partial(_ffn_kernel, dff=dff, fchunk=fchunk, nsub=nsub, epilogue=epilogue)
    row = lambda i: (i, 0)
    out = pl.pallas_call(
        kern,
        out_shape=[jax.ShapeDtypeStruct((n, d), _F32)] + [jax.ShapeDtypeStruct((n, d), _BF16)] * emit,
        grid=(n // tm,),
        in_specs=[
            pl.BlockSpec((tm, d), row),
            pl.BlockSpec((tm, d), row),
            _resident((1, d)),
            _resident((d, d)),
            _resident((d, dff)),
            _resident((d, dff)),
            _resident((dff, d)),
            _resident((1, d)),
        ],
        out_specs=[pl.BlockSpec((tm, d), row)] * (1 + emit),
        scratch_shapes=[
            pltpu.VMEM((tm, d), _BF16),
            pltpu.VMEM((tm, dff), _BF16),
        ],
        compiler_params=_params("parallel"),
        name="outproj_swiglu",
    )(h, m, gn, wo, wg, wu, wd, ge)
    return tuple(out) if emit else out[0]


def kernel(x, conv_w_in, conv_k, conv_w_out, fourier_w_out, mix_norm_g, ffn_norm_g,
           ffn_w_gate, ffn_w_up, ffn_w_down, final_norm_g):
    bsz, seq, d = x.shape
    depth = mix_norm_g.shape[0]
    dff = ffn_w_gate.shape[-1]

    tm_conv = 1024
    tm_ffn = 1024
    ffn_nsub = 2
    fchunk = 256
    assert seq % tm_conv == 0 and (bsz * seq) % tm_ffn == 0 and dff % fchunk == 0

    jb = 4
    kb = 4
    n1, n2, n1c, l1, c2, s2, twc, tws, cs, perm = _dft_tables(seq, d // FOURIER_GROUPS, kb)
    l1, cs, perm = l1.astype(_BF16), cs.astype(_BF16), perm.astype(_BF16)
    assert d % FOURIER_GROUPS == 0 and seq == n1 * n2
    assert n2 % (V7X_BF16_SUBLANES * jb) == 0 and n1 % V7X_BF16_SUBLANES == 0

    gfin = final_norm_g.reshape(1, d)
    is_conv = lambda i: i % 2 == 0

    def ffn_casts(i):
        wo = (conv_w_out, i // 2) if is_conv(i) else (fourier_w_out, i // 2)
        return [wo, (ffn_w_gate, i), (ffn_w_up, i), (ffn_w_down, i)]

    def mixer_casts(i):
        nxt = ffn_casts(i + 1) if i + 1 < depth else []
        w_in_next = [(conv_w_in, (i + 2) // 2)] if i + 2 < depth else []
        return ffn_casts(i) + nxt + w_in_next

    w_in = conv_w_in[0].astype(_BF16)
    ffn_w = {}
    h = x
    for i in range(depth):
        j = i // 2
        gmix = mix_norm_g[i].reshape(1, d)
        if is_conv(i):
            m, cast = _conv_mixer(h, gmix, w_in, conv_k[j], mixer_casts(i), tm=tm_conv)
            ffn_w[i], cast = cast[:4], cast[4:]
            if i + 1 < depth:
                ffn_w[i + 1], cast = cast[:4], cast[4:]
            if i + 2 < depth:
                w_in, = cast
        else:
            t1 = _fourier_stage1(xn.reshape(bsz, seq, d), l1, n1=n1, n1c=n1c, n2=n2, jb=jb)
            m = _fourier_stage2(t1, c2, s2, twc, tws, cs, perm, n1=n1, n1c=n1c, n2=n2, d=d, kb=kb)
        if i == depth - 1:
            epilogue, ge = "final_norm", gfin
        elif not is_conv(i + 1):
            epilogue, ge = "emit_norm", mix_norm_g[i + 1].reshape(1, d)
        else:
            epilogue, ge = "residual", gfin
        out = _ffn(h.reshape(bsz * seq, d), m.reshape(bsz * seq, d), ffn_norm_g[i].reshape(1, d),
                   *ffn_w[i], ge, tm=tm_ffn, fchunk=fchunk, nsub=ffn_nsub, epilogue=epilogue)
        h, xn = out if epilogue == "emit_norm" else (out, None)
        h = h.reshape(bsz, seq, d)
    return h
```

```python
import functools

import numpy as np
import jax
import jax.numpy as jnp
from jax import lax
from jax.experimental import pallas as pl
from jax.experimental.pallas import tpu as pltpu

RMS_EPS = 1e-5
FOURIER_GROUPS = 8

V7X_LANES = 128
V7X_F32_SUBLANES = 8
V7X_BF16_SUBLANES = 16
V7X_MXU_DIM = 256
V7X_VMEM_LIMIT_BYTES = 56 * 1024 * 1024

_BF16 = jnp.bfloat16
_F32 = jnp.float32


def _dot(a, b):
    return jnp.dot(a, b, preferred_element_type=_F32)


def _rmsnorm(x, g):
    r = lax.rsqrt(jnp.mean(x * x, axis=-1, keepdims=True) + RMS_EPS)
    return x * r * g


def _resident(shape):
    return pl.BlockSpec(shape, lambda *_: (0,) * len(shape), pipeline_mode=pl.Buffered(1))


def _params(*semantics):
    return pltpu.CompilerParams(dimension_semantics=semantics,
                                vmem_limit_bytes=V7X_VMEM_LIMIT_BYTES)


def _cast_specs(stacked, layer, steps, step_of):
    _, rows, cols = stacked.shape
    nblk = steps
    while rows % (nblk * V7X_BF16_SUBLANES):
        nblk //= 2
    rb, every = rows // nblk, steps // nblk
    in_spec = pl.BlockSpec((None, rb, cols), lambda *g: (layer, step_of(*g) // every, 0))
    out_spec = pl.BlockSpec((rb, cols), lambda *g: (step_of(*g) // every, 0))
    return in_spec, out_spec, jax.ShapeDtypeStruct((rows, cols), _BF16)


def _with_casts(body, n_in, n_out, n_cast):
    def kern(*refs):
        ins, rest = refs[:n_in], refs[n_in:]
        cast_in, rest = rest[:n_cast], rest[n_cast:]
        outs, rest = rest[:n_out], rest[n_out:]
        cast_out, scratch = rest[:n_cast], rest[n_cast:]
        for src, dst in zip(cast_in, cast_out):
            dst[...] = src[...].astype(_BF16)
        body(*ins, *outs, *scratch)
    return kern


def _conv_mixer_kernel(h_ref, hprev_ref, hnext_ref, g_ref, win_ref, k_ref, m_ref, *, tm, d):
    halo = V7X_F32_SUBLANES
    i = pl.program_id(1)
    g = g_ref[...]
    xn = jnp.concatenate([_rmsnorm(hnext_ref[...], g), _rmsnorm(hprev_ref[...], g),
                          _rmsnorm(h_ref[...], g)], axis=0).astype(_BF16)
    u = _dot(xn, win_ref[:, d:2 * d]) * _dot(xn, win_ref[:, 2 * d:3 * d])
    u_next = jnp.where(i == pl.num_programs(1) - 1, 0.0, u[0:halo])
    u_prev = jnp.where(i == 0, 0.0, u[halo:2 * halo])
    u_mid = u[2 * halo:]
    slab = jnp.concatenate([u_prev, u_mid, u_next], axis=0)
    up = pltpu.roll(slab, 1, 0)[halo:halo + tm]
    dn = pltpu.roll(slab, tm + 2 * halo - 1, 0)[halo:halo + tm]
    conv = k_ref[0:1] * up + k_ref[1:2] * u_mid + k_ref[2:3] * dn
    b = _dot(xn[2 * halo:], win_ref[:, 0:d])
    m_ref[...] = (b * conv).astype(_BF16)


def _conv_mixer(h, g, w_in, k, casts, *, tm):
    bsz, seq, d = h.shape
    halo = V7X_F32_SUBLANES
    nt = seq // tm
    hb = tm // halo
    last = seq // halo - 1
    cspecs = [_cast_specs(w, l, bsz * nt, lambda b, i: b * nt + i) for w, l in casts]
    body = functools.partial(_conv_mixer_kernel, tm=tm, d=d)
    m, *cast_out = pl.pallas_call(
        _with_casts(body, 6, 1, len(casts)),
        out_shape=[jax.ShapeDtypeStruct((bsz, seq, d), _BF16)] + [c[2] for c in cspecs],
        grid=(bsz, nt),
        in_specs=[
            pl.BlockSpec((None, tm, d), lambda b, i: (b, i, 0)),
            pl.BlockSpec((None, halo, d), lambda b, i: (b, jnp.maximum(i * hb - 1, 0), 0)),
            pl.BlockSpec((None, halo, d), lambda b, i: (b, jnp.minimum((i + 1) * hb, last), 0)),
            _resident((1, d)),
            _resident((d, 3 * d)),
            _resident((3, d)),
        ] + [c[0] for c in cspecs],
        out_specs=[pl.BlockSpec((None, tm, d), lambda b, i: (b, i, 0))] + [c[1] for c in cspecs],
        compiler_params=_params("arbitrary", "arbitrary"),
        name="conv_mixer",
    )(h, h, h, g, w_in, k, *[w for w, _ in casts])
    return m, cast_out


def _dft_tables(seq, gd, kb):
    n1 = V7X_MXU_DIM // V7X_F32_SUBLANES
    n2 = seq // n1
    n1c = n1 // 2 + 1
    grp = V7X_BF16_SUBLANES
    two_pi = 2.0 * np.pi
    th1 = two_pi * np.outer(np.arange(n1c), np.arange(n1)) / n1
    f1 = np.stack([np.cos(th1), -np.sin(th1)], axis=1).reshape(2 * n1c, n1) / np.sqrt(n1)
    l1 = np.kron(f1, np.eye(V7X_F32_SUBLANES))
    th2 = two_pi * np.outer(np.arange(n2), np.arange(n2)) / n2
    c2 = np.cos(th2) / np.sqrt(n2)
    s2 = np.sin(th2) / np.sqrt(n2)
    tw = (two_pi * np.outer(np.arange(n1c), np.arange(n2)) / seq).reshape(n1c, 1, n2)
    ph = two_pi * np.outer(np.arange(gd), np.arange(gd)) / gd
    cs = np.block([[np.cos(ph), np.cos(ph)], [np.sin(ph), -np.sin(ph)]]) / np.sqrt(gd)
    a, b = np.meshgrid(np.arange(grp), np.arange(grp), indexing="ij")
    perm = np.zeros((2, grp * grp, grp * grp))
    perm[0, (b * grp + a).ravel(), (a * grp + b).ravel()] = 1.0
    perm[1, (b * grp + a).ravel(), (a * grp + np.where(a == 0, b, grp - 1 - b)).ravel()] = 1.0
    f = lambda t: jnp.asarray(t, _F32)
    return n1, n2, n1c, f(l1), f(c2), f(s2), f(np.cos(tw)), f(np.sin(tw)), f(cs), f(perm)


def _fourier_stage1_kernel(h_ref, g_ref, l1_ref, t_ref, *, n1, n1c, jb, d):
    g = g_ref[...]
    l1 = l1_ref[...]
    sub = V7X_F32_SUBLANES
    for j in range(jb):
        halves = []
        for half in range(V7X_BF16_SUBLANES // sub):
            x = h_ref[:, j, half * sub:(half + 1) * sub, :].reshape(n1 * sub, d)
            xn = _rmsnorm(x, g).astype(_BF16)
            halves.append(_dot(l1, xn).reshape(2 * n1c, sub, d))
        t = jnp.concatenate(halves, axis=1).astype(_BF16)
        t_ref[:, :, j] = t.reshape(n1c, 2, V7X_BF16_SUBLANES, d)


def _fourier_stage1(h, g, l1, *, n1, n1c, n2, jb):
    bsz, seq, d = h.shape
    grp = V7X_BF16_SUBLANES
    ng = n2 // grp
    kern = functools.partial(_fourier_stage1_kernel, n1=n1, n1c=n1c, jb=jb, d=d)
    return pl.pallas_call(
        kern,
        out_shape=jax.ShapeDtypeStruct((bsz, n1c, 2, ng, grp, d), _BF16),
        grid=(bsz, ng // jb),
        in_specs=[
            pl.BlockSpec((None, n1, jb, grp, d), lambda b, j: (b, 0, j, 0, 0)),
            _resident((1, d)),
            _resident(l1.shape),
        ],
        out_specs=pl.BlockSpec((None, n1c, 2, jb, grp, d), lambda b, j: (b, 0, 0, j, 0, 0)),
        compiler_params=_params("parallel", "parallel"),
        name="fourier_stage1",
    )(h.reshape(bsz, n1, ng, grp, d), g, l1)


def _fourier_stage2_kernel(t_ref, c2_ref, s2_ref, twc_ref, tws_ref, cs_ref, p_ref, y_ref,
                           ylo_scr, yhi_scr, *, kb, n2, d, gd):
    i = pl.program_id(1)
    grp = V7X_BF16_SUBLANES
    nmain = grp // kb

    def slab(kk):
        twc = twc_ref[kk]
        tws = tws_ref[kk]
        cth = (c2_ref[...] * twc - s2_ref[...] * tws).astype(_BF16)
        sth = s2_ref[...] * twc + c2_ref[...] * tws
        gmat = jnp.concatenate(
            [jnp.concatenate([cth, sth.astype(_BF16)], axis=1),
             jnp.concatenate([(-sth).astype(_BF16), cth], axis=1)], axis=0)
        u = _dot(gmat, t_ref[kk]).astype(_BF16)
        out = []
        for gi in range(d // gd):
            gsl = slice(gi * gd, (gi + 1) * gd)
            lhs = jnp.concatenate([u[0:n2, gsl], u[n2:2 * n2, gsl]], axis=1)
            out.append(_dot(lhs, cs_ref[...]))
        return out

    def interleave(src_scr, perm, mirrored):
        nblk = n2 // grp
        for k2h in range(nblk):
            pieces = []
            for k1l in range(grp):
                blk = nblk - 1 - k2h if (mirrored and k1l > 0) else k2h
                pieces.append(src_scr[k1l, blk * grp:(blk + 1) * grp, :])
            res = _dot(perm, jnp.concatenate(pieces, axis=0))
            y_ref[k2h * grp:(k2h + 1) * grp] = res.astype(_BF16).reshape(grp, grp, d)

    @pl.when(i < nmain)
    def _():
        for kk in range(kb):
            k1 = i * kb + kk
            mirror = (grp - k1) % grp
            for gi, pq in enumerate(slab(kk)):
                gsl = slice(gi * gd, (gi + 1) * gd)
                ylo_scr[k1, :, gsl] = pq[:, 0:gd].astype(_BF16)
                yhi_scr[mirror, :, gsl] = pq[:, gd:2 * gd].astype(_BF16)

    @pl.when(i == nmain - 1)
    def _():
        interleave(ylo_scr, p_ref[0], mirrored=False)

    @pl.when(i == nmain)
    def _():
        for gi, pq in enumerate(slab(0)):
            yhi_scr[0, :, gi * gd:(gi + 1) * gd] = pq[:, 0:gd].astype(_BF16)
        interleave(yhi_scr, p_ref[1], mirrored=True)


def _fourier_stage2(t1, c2, s2, twc, tws, cs, perm, *, n1, n1c, n2, d, kb):
    bsz = t1.shape[0]
    gd = d // FOURIER_GROUPS
    grp = V7X_BF16_SUBLANES
    nmain = grp // kb
    assert n1 == 2 * grp and n1c == nmain * kb + 1
    kern = functools.partial(_fourier_stage2_kernel, kb=kb, n2=n2, d=d, gd=gd)
    return pl.pallas_call(
        kern,
        out_shape=jax.ShapeDtypeStruct((bsz, n2, n1 // grp, grp, d), _BF16),
        grid=(bsz, nmain + 1),
        in_specs=[
            pl.BlockSpec((None, kb, 2 * n2, d), lambda b, i: (b, i, 0, 0)),
            _resident((n2, n2)),
            _resident((n2, n2)),
            pl.BlockSpec((kb, 1, n2), lambda b, i: (i, 0, 0)),
            pl.BlockSpec((kb, 1, n2), lambda b, i: (i, 0, 0)),
            _resident(cs.shape),
            _resident(perm.shape),
        ],
        out_specs=pl.BlockSpec((None, n2, None, grp, d), lambda b, i: (b, 0, i // nmain, 0, 0)),
        scratch_shapes=[
            pltpu.VMEM((grp, n2, d), _BF16),
            pltpu.VMEM((grp, n2, d), _BF16),
        ],
        compiler_params=_params("arbitrary", "arbitrary"),
        name="fourier_stage2",
    )(t1.reshape(bsz, n1c, 2 * n2, d), c2, s2, twc, tws, cs, perm)


def _ffn_kernel(h_ref, m_ref, gn_ref, wo_ref, wg_ref, wu_ref, wd_ref, gf_ref, o_ref,
                hn_scr, act_scr, *, dff, fchunk, nsub, final):
    rs = h_ref.shape[0] // nsub
    for q in range(nsub):
        rows = slice(q * rs, (q + 1) * rs)
        h1 = h_ref[rows] + _dot(m_ref[rows], wo_ref[...])
        o_ref[rows] = h1
        hn_scr[rows] = _rmsnorm(h1, gn_ref[...]).astype(_BF16)
    for q in range(nsub):
        rows = slice(q * rs, (q + 1) * rs)
        for c0 in range(0, dff, fchunk):
            a = _dot(hn_scr[rows], wg_ref[:, c0:c0 + fchunk])
            b = _dot(hn_scr[rows], wu_ref[:, c0:c0 + fchunk])
            act_scr[rows, c0:c0 + fchunk] = (a * jax.nn.sigmoid(a) * b).astype(_BF16)
        out = o_ref[rows] + _dot(act_scr[rows], wd_ref[...])
        if final:
            out = _rmsnorm(out, gf_ref[...])
        o_ref[rows] = out


def _ffn(h, m, gn, wo, wg, wu, wd, gf, *, tm, fchunk, nsub, final):
    n, d = h.shape
    dff = wg.shape[1]
    kern = functools.partial(_ffn_kernel, dff=dff, fchunk=fchunk, nsub=nsub, final=final)
    row = lambda i: (i, 0)
    return pl.pallas_call(
        kern,
        out_shape=jax.ShapeDtypeStruct((n, d), _F32),
        grid=(n // tm,),
        in_specs=[
            pl.BlockSpec((tm, d), row),
            pl.BlockSpec((tm, d), row),
            _resident((1, d)),
            _resident((d, d)),
            _resident((d, dff)),
            _resident((d, dff)),
            _resident((dff, d)),
            _resident((1, d)),
        ],
        out_specs=pl.BlockSpec((tm, d), row),
        scratch_shapes=[
            pltpu.VMEM((tm, d), _BF16),
            pltpu.VMEM((tm, dff), _BF16),
        ],
        compiler_params=_params("parallel"),
        name="outproj_swiglu",
    )(h, m, gn, wo, wg, wu, wd, gf)


def kernel(x, conv_w_in, conv_k, conv_w_out, fourier_w_out, mix_norm_g, ffn_norm_g,
           ffn_w_gate, ffn_w_up, ffn_w_down, final_norm_g):
    bsz, seq, d = x.shape
    depth = mix_norm_g.shape[0]
    dff = ffn_w_gate.shape[-1]

    tm_conv = 1024
    tm_ffn = 1024
    ffn_nsub = 2
    fchunk = 256
    assert seq % tm_conv == 0 and (bsz * seq) % tm_ffn == 0 and dff % fchunk == 0

    jb = 4
    kb = 4
    n1, n2, n1c, l1, c2, s2, twc, tws, cs, perm = _dft_tables(seq, d // FOURIER_GROUPS, kb)
    l1, cs, perm = l1.astype(_BF16), cs.astype(_BF16), perm.astype(_BF16)
    assert d % FOURIER_GROUPS == 0 and seq == n1 * n2
    assert n2 % (V7X_BF16_SUBLANES * jb) == 0 and n1 % V7X_BF16_SUBLANES == 0

    gfin = final_norm_g.reshape(1, d)
    is_conv = lambda i: i % 2 == 0

    def ffn_casts(i):
        wo = (conv_w_out, i // 2) if is_conv(i) else (fourier_w_out, i // 2)
        return [wo, (ffn_w_gate, i), (ffn_w_up, i), (ffn_w_down, i)]

    def mixer_casts(i):
        nxt = ffn_casts(i + 1) if i + 1 < depth else []
        w_in_next = [(conv_w_in, (i + 2) // 2)] if i + 2 < depth else []
        return ffn_casts(i) + nxt + w_in_next

    w_in = conv_w_in[0].astype(_BF16)
    ffn_w = {}
    h = x
    for i in range(depth):
        j = i // 2
        gmix = mix_norm_g[i].reshape(1, d)
        if is_conv(i):
            m, cast = _conv_mixer(h, gmix, w_in, conv_k[j], mixer_casts(i), tm=tm_conv)
            ffn_w[i], cast = cast[:4], cast[4:]
            if i + 1 < depth:
                ffn_w[i + 1], cast = cast[:4], cast[4:]
            if i + 2 < depth:
                w_in, = cast
        else:
            t1 = _fourier_stage1(h, gmix, l1, n1=n1, n1c=n1c, n2=n2, jb=jb)
            m = _fourier_stage2(t1, c2, s2, twc, tws, cs, perm, n1=n1, n1c=n1c, n2=n2, d=d, kb=kb)
        h = _ffn(h.reshape(bsz * seq, d), m.reshape(bsz * seq, d), ffn_norm_g[i].reshape(1, d),
                 *ffn_w[i], gfin, tm=tm_ffn, fchunk=fchunk, nsub=ffn_nsub, final=(i == depth - 1))
        h = h.reshape(bsz, seq, d)
    return h
```

```python
import functools

import numpy as np
import jax
import jax.numpy as jnp
from jax import lax
from jax.experimental import pallas as pl
from jax.experimental.pallas import tpu as pltpu

RMS_EPS = 1e-5
FOURIER_GROUPS = 8

V7X_LANES = 128
V7X_F32_SUBLANES = 8
V7X_BF16_SUBLANES = 16
V7X_MXU_DIM = 256
V7X_VMEM_LIMIT_BYTES = 56 * 1024 * 1024

_BF16 = jnp.bfloat16
_F32 = jnp.float32


def _dot(a, b):
    return jnp.dot(a, b, preferred_element_type=_F32)


def _rmsnorm(x, g):
    r = lax.rsqrt(jnp.mean(x * x, axis=-1, keepdims=True) + RMS_EPS)
    return x * r * g


def _resident(shape):
    return pl.BlockSpec(shape, lambda *_: (0,) * len(shape), pipeline_mode=pl.Buffered(1))


def _params(*semantics):
    return pltpu.CompilerParams(dimension_semantics=semantics,
                                vmem_limit_bytes=V7X_VMEM_LIMIT_BYTES)


def _cast_specs(stacked, layer, steps, step_of):
    _, rows, cols = stacked.shape
    nblk = steps
    while rows % (nblk * V7X_BF16_SUBLANES):
        nblk //= 2
    rb, every = rows // nblk, steps // nblk
    in_spec = pl.BlockSpec((None, rb, cols), lambda *g: (layer, step_of(*g) // every, 0))
    out_spec = pl.BlockSpec((rb, cols), lambda *g: (step_of(*g) // every, 0))
    return in_spec, out_spec, jax.ShapeDtypeStruct((rows, cols), _BF16)


def _with_casts(body, n_in, n_out, n_cast):
    def kern(*refs):
        ins, rest = refs[:n_in], refs[n_in:]
        cast_in, rest = rest[:n_cast], rest[n_cast:]
        outs, rest = rest[:n_out], rest[n_out:]
        cast_out, scratch = rest[:n_cast], rest[n_cast:]
        for src, dst in zip(cast_in, cast_out):
            dst[...] = src[...].astype(_BF16)
        body(*ins, *outs, *scratch)
    return kern


def _conv_mixer_kernel(h_ref, hprev_ref, hnext_ref, g_ref, win_ref, k_ref, m_ref, *wbf, tm, d):
    halo = V7X_F32_SUBLANES
    i = pl.program_id(1)
    if wbf:
        @pl.when((pl.program_id(0) == 0) & (i == 0))
        def _():
            wbf[0][...] = win_ref[...].astype(_BF16)
        win_ref = wbf[0]
    g = g_ref[...]
    xn = jnp.concatenate([_rmsnorm(hnext_ref[...], g), _rmsnorm(hprev_ref[...], g),
                          _rmsnorm(h_ref[...], g)], axis=0).astype(_BF16)
    u = _dot(xn, win_ref[:, d:2 * d]) * _dot(xn, win_ref[:, 2 * d:3 * d])
    u_next = jnp.where(i == pl.num_programs(1) - 1, 0.0, u[0:halo])
    u_prev = jnp.where(i == 0, 0.0, u[halo:2 * halo])
    u_mid = u[2 * halo:]
    slab = jnp.concatenate([u_prev, u_mid, u_next], axis=0)
    up = pltpu.roll(slab, 1, 0)[halo:halo + tm]
    dn = pltpu.roll(slab, tm + 2 * halo - 1, 0)[halo:halo + tm]
    conv = k_ref[0:1] * up + k_ref[1:2] * u_mid + k_ref[2:3] * dn
    b = _dot(xn[2 * halo:], win_ref[:, 0:d])
    m_ref[...] = (b * conv).astype(_BF16)


def _conv_mixer(h, g, w_in, k, casts, *, tm):
    bsz, seq, d = h.shape
    halo = V7X_F32_SUBLANES
    nt = seq // tm
    hb = tm // halo
    last = seq // halo - 1
    cspecs = [_cast_specs(w, l, bsz * nt, lambda b, i: b * nt + i) for w, l in casts]
    body = functools.partial(_conv_mixer_kernel, tm=tm, d=d)
    if w_in.dtype == _F32:
        w_spec = pl.BlockSpec((None, d, 3 * d), lambda b, i: (0, 0, 0), pipeline_mode=pl.Buffered(1))
        scratch = [pltpu.VMEM((d, 3 * d), _BF16)]
    else:
        w_spec, scratch = _resident((d, 3 * d)), []
    m, *cast_out = pl.pallas_call(
        _with_casts(body, 6, 1, len(casts)),
        out_shape=[jax.ShapeDtypeStruct((bsz, seq, d), _BF16)] + [c[2] for c in cspecs],
        grid=(bsz, nt),
        in_specs=[
            pl.BlockSpec((None, tm, d), lambda b, i: (b, i, 0)),
            pl.BlockSpec((None, halo, d), lambda b, i: (b, jnp.maximum(i * hb - 1, 0), 0)),
            pl.BlockSpec((None, halo, d), lambda b, i: (b, jnp.minimum((i + 1) * hb, last), 0)),
            _resident((1, d)),
            w_spec,
            _resident((3, d)),
        ] + [c[0] for c in cspecs],
        out_specs=[pl.BlockSpec((None, tm, d), lambda b, i: (b, i, 0))] + [c[1] for c in cspecs],
        scratch_shapes=scratch,
        compiler_params=_params("arbitrary", "arbitrary"),
        name="conv_mixer",
    )(h, h, h, g, w_in, k, *[w for w, _ in casts])
    return m, cast_out


def _dft_tables(seq, gd, kb):
    n1 = V7X_MXU_DIM // V7X_F32_SUBLANES
    n2 = seq // n1
    n1c = n1 // 2 + 1
    grp = V7X_BF16_SUBLANES
    two_pi = 2.0 * np.pi
    th1 = two_pi * np.outer(np.arange(n1c), np.arange(n1)) / n1
    f1 = np.stack([np.cos(th1), -np.sin(th1)], axis=1).reshape(2 * n1c, n1) / np.sqrt(n1)
    l1 = np.kron(f1, np.eye(V7X_F32_SUBLANES))
    th2 = two_pi * np.outer(np.arange(n2), np.arange(n2)) / n2
    c2 = np.cos(th2) / np.sqrt(n2)
    s2 = np.sin(th2) / np.sqrt(n2)
    tw = (two_pi * np.outer(np.arange(n1c), np.arange(n2)) / seq).reshape(n1c, 1, n2)
    ph = two_pi * np.outer(np.arange(gd), np.arange(gd)) / gd
    cs = np.block([[np.cos(ph), np.cos(ph)], [np.sin(ph), -np.sin(ph)]]) / np.sqrt(gd)
    a, b = np.meshgrid(np.arange(grp), np.arange(grp), indexing="ij")
    perm = np.zeros((2, grp * grp, grp * grp))
    perm[0, (b * grp + a).ravel(), (a * grp + b).ravel()] = 1.0
    perm[1, (b * grp + a).ravel(), (a * grp + np.where(a == 0, b, grp - 1 - b)).ravel()] = 1.0
    f = lambda t: jnp.asarray(t, _F32)
    return n1, n2, n1c, f(l1), f(c2), f(s2), f(np.cos(tw)), f(np.sin(tw)), f(cs), f(perm)


def _fourier_stage1_kernel(h_ref, g_ref, l1_ref, t_ref, *, n1, n1c, jb, d):
    g = g_ref[...]
    l1 = l1_ref[...]
    sub = V7X_F32_SUBLANES
    for j in range(jb):
        halves = []
        for half in range(V7X_BF16_SUBLANES // sub):
            x = h_ref[:, j, half * sub:(half + 1) * sub, :].reshape(n1 * sub, d)
            xn = _rmsnorm(x, g).astype(_BF16)
            halves.append(_dot(l1, xn).reshape(2 * n1c, sub, d))
        t = jnp.concatenate(halves, axis=1).astype(_BF16)
        t_ref[:, :, j] = t.reshape(n1c, 2, V7X_BF16_SUBLANES, d)


def _fourier_stage1(h, g, l1, *, n1, n1c, n2, jb):
    bsz, seq, d = h.shape
    grp = V7X_BF16_SUBLANES
    ng = n2 // grp
    kern = functools.partial(_fourier_stage1_kernel, n1=n1, n1c=n1c, jb=jb, d=d)
    return pl.pallas_call(
        kern,
        out_shape=jax.ShapeDtypeStruct((bsz, n1c, 2, ng, grp, d), _BF16),
        grid=(bsz, ng // jb),
        in_specs=[
            pl.BlockSpec((None, n1, jb, grp, d), lambda b, j: (b, 0, j, 0, 0)),
            _resident((1, d)),
            _resident(l1.shape),
        ],
        out_specs=pl.BlockSpec((None, n1c, 2, jb, grp, d), lambda b, j: (b, 0, 0, j, 0, 0)),
        compiler_params=_params("parallel", "parallel"),
        name="fourier_stage1",
    )(h.reshape(bsz, n1, ng, grp, d), g, l1)


def _fourier_stage2_kernel(t_ref, c2_ref, s2_ref, twc_ref, tws_ref, cs_ref, p_ref, y_ref,
                           ylo_scr, yhi_scr, *, kb, n2, d, gd):
    i = pl.program_id(1)
    grp = V7X_BF16_SUBLANES
    nmain = grp // kb

    def slab(kk):
        twc = twc_ref[kk]
        tws = tws_ref[kk]
        cth = (c2_ref[...] * twc - s2_ref[...] * tws).astype(_BF16)
        sth = s2_ref[...] * twc + c2_ref[...] * tws
        gmat = jnp.concatenate(
            [jnp.concatenate([cth, sth.astype(_BF16)], axis=1),
             jnp.concatenate([(-sth).astype(_BF16), cth], axis=1)], axis=0)
        u = _dot(gmat, t_ref[kk]).astype(_BF16)
        out = []
        for gi in range(d // gd):
            gsl = slice(gi * gd, (gi + 1) * gd)
            lhs = jnp.concatenate([u[0:n2, gsl], u[n2:2 * n2, gsl]], axis=1)
            out.append(_dot(lhs, cs_ref[...]))
        return out

    def interleave(src_scr, perm, mirrored):
        nblk = n2 // grp
        for k2h in range(nblk):
            pieces = []
            for k1l in range(grp):
                blk = nblk - 1 - k2h if (mirrored and k1l > 0) else k2h
                pieces.append(src_scr[k1l, blk * grp:(blk + 1) * grp, :])
            res = _dot(perm, jnp.concatenate(pieces, axis=0))
            y_ref[k2h * grp:(k2h + 1) * grp] = res.astype(_BF16).reshape(grp, grp, d)

    @pl.when(i < nmain)
    def _():
        for kk in range(kb):
            k1 = i * kb + kk
            mirror = (grp - k1) % grp
            for gi, pq in enumerate(slab(kk)):
                gsl = slice(gi * gd, (gi + 1) * gd)
                ylo_scr[k1, :, gsl] = pq[:, 0:gd].astype(_BF16)
                yhi_scr[mirror, :, gsl] = pq[:, gd:2 * gd].astype(_BF16)

    @pl.when(i == nmain - 1)
    def _():
        interleave(ylo_scr, p_ref[0], mirrored=False)

    @pl.when(i == nmain)
    def _():
        for gi, pq in enumerate(slab(0)):
            yhi_scr[0, :, gi * gd:(gi + 1) * gd] = pq[:, 0:gd].astype(_BF16)
        interleave(yhi_scr, p_ref[1], mirrored=True)


def _fourier_stage2(t1, c2, s2, twc, tws, cs, perm, *, n1, n1c, n2, d, kb):
    bsz = t1.shape[0]
    gd = d // FOURIER_GROUPS
    grp = V7X_BF16_SUBLANES
    nmain = grp // kb
    assert n1 == 2 * grp and n1c == nmain * kb + 1
    kern = functools.partial(_fourier_stage2_kernel, kb=kb, n2=n2, d=d, gd=gd)
    return pl.pallas_call(
        kern,
        out_shape=jax.ShapeDtypeStruct((bsz, n2, n1 // grp, grp, d), _BF16),
        grid=(bsz, nmain + 1),
        in_specs=[
            pl.BlockSpec((None, kb, 2 * n2, d), lambda b, i: (b, i, 0, 0)),
            _resident((n2, n2)),
            _resident((n2, n2)),
            pl.BlockSpec((kb, 1, n2), lambda b, i: (i, 0, 0)),
            pl.BlockSpec((kb, 1, n2), lambda b, i: (i, 0, 0)),
            _resident(cs.shape),
            _resident(perm.shape),
        ],
        out_specs=pl.BlockSpec((None, n2, None, grp, d), lambda b, i: (b, 0, i // nmain, 0, 0)),
        scratch_shapes=[
            pltpu.VMEM((grp, n2, d), _BF16),
            pltpu.VMEM((grp, n2, d), _BF16),
        ],
        compiler_params=_params("arbitrary", "arbitrary"),
        name="fourier_stage2",
    )(t1.reshape(bsz, n1c, 2 * n2, d), c2, s2, twc, tws, cs, perm)


def _ffn_kernel(h_ref, m_ref, gn_ref, wo_ref, wg_ref, wu_ref, wd_ref, gf_ref, o_ref,
                hn_scr, act_scr, *, dff, fchunk, nsub, final):
    rs = h_ref.shape[0] // nsub
    for q in range(nsub):
        rows = slice(q * rs, (q + 1) * rs)
        h1 = h_ref[rows] + _dot(m_ref[rows], wo_ref[...])
        o_ref[rows] = h1
        hn_scr[rows] = _rmsnorm(h1, gn_ref[...]).astype(_BF16)
    for q in range(nsub):
        rows = slice(q * rs, (q + 1) * rs)
        for c0 in range(0, dff, fchunk):
            a = _dot(hn_scr[rows], wg_ref[:, c0:c0 + fchunk])
            b = _dot(hn_scr[rows], wu_ref[:, c0:c0 + fchunk])
            act_scr[rows, c0:c0 + fchunk] = (a * jax.nn.sigmoid(a) * b).astype(_BF16)
        out = o_ref[rows] + _dot(act_scr[rows], wd_ref[...])
        if final:
            out = _rmsnorm(out, gf_ref[...])
        o_ref[rows] = out


def _ffn(h, m, gn, wo, wg, wu, wd, gf, *, tm, fchunk, nsub, final):
    n, d = h.shape
    dff = wg.shape[1]
    kern = functools.partial(_ffn_kernel, dff=dff, fchunk=fchunk, nsub=nsub, final=final)
    row = lambda i: (i, 0)
    return pl.pallas_call(
        kern,
        out_shape=jax.ShapeDtypeStruct((n, d), _F32),
        grid=(n // tm,),
        in_specs=[
            pl.BlockSpec((tm, d), row),
            pl.BlockSpec((tm, d), row),
            _resident((1, d)),
            _resident((d, d)),
            _resident((d, dff)),
            _resident((d, dff)),
            _resident((dff, d)),
            _resident((1, d)),
        ],
        out_specs=pl.BlockSpec((tm, d), row),
        scratch_shapes=[
            pltpu.VMEM((tm, d), _BF16),
            pltpu.VMEM((tm, dff), _BF16),
        ],
        compiler_params=_params("parallel"),
        name="outproj_swiglu",
    )(h, m, gn, wo, wg, wu, wd, gf)


def kernel(x, conv_w_in, conv_k, conv_w_out, fourier_w_out, mix_norm_g, ffn_norm_g,
           ffn_w_gate, ffn_w_up, ffn_w_down, final_norm_g):
    bsz, seq, d = x.shape
    depth = mix_norm_g.shape[0]
    dff = ffn_w_gate.shape[-1]

    tm_conv = 1024
    tm_ffn = 1024
    ffn_nsub = 2
    fchunk = 256
    assert seq % tm_conv == 0 and (bsz * seq) % tm_ffn == 0 and dff % fchunk == 0

    jb = 4
    kb = 4
    n1, n2, n1c, l1, c2, s2, twc, tws, cs, perm = _dft_tables(seq, d // FOURIER_GROUPS, kb)
    l1, cs, perm = l1.astype(_BF16), cs.astype(_BF16), perm.astype(_BF16)
    assert d % FOURIER_GROUPS == 0 and seq == n1 * n2
    assert n2 % (V7X_BF16_SUBLANES * jb) == 0 and n1 % V7X_BF16_SUBLANES == 0

    gfin = final_norm_g.reshape(1, d)
    is_conv = lambda i: i % 2 == 0

    def ffn_casts(i):
        wo = (conv_w_out, i // 2) if is_conv(i) else (fourier_w_out, i // 2)
        return [wo, (ffn_w_gate, i), (ffn_w_up, i), (ffn_w_down, i)]

    def mixer_casts(i):
        nxt = ffn_casts(i + 1) if i + 1 < depth else []
        w_in_next = [(conv_w_in, (i + 2) // 2)] if i + 2 < depth else []
        return ffn_casts(i) + nxt + w_in_next

    w_in = conv_w_in
    ffn_w = {}
    h = x
    for i in range(depth):
        j = i // 2
        gmix = mix_norm_g[i].reshape(1, d)
        if is_conv(i):
            m, cast = _conv_mixer(h, gmix, w_in, conv_k[j], mixer_casts(i), tm=tm_conv)
            ffn_w[i], cast = cast[:4], cast[4:]
            if i + 1 < depth:
                ffn_w[i + 1], cast = cast[:4], cast[4:]
            if i + 2 < depth:
                w_in, = cast
        else:
            t1 = _fourier_stage1(h, gmix, l1, n1=n1, n1c=n1c, n2=n2, jb=jb)
            m = _fourier_stage2(t1, c2, s2, twc, tws, cs, perm, n1=n1, n1c=n1c, n2=n2, d=d, kb=kb)
        h = _ffn(h.reshape(bsz * seq, d), m.reshape(bsz * seq, d), ffn_norm_g[i].reshape(1, d),
                 *ffn_w[i], gfin, tm=tm_ffn, fchunk=fchunk, nsub=ffn_nsub, final=(i == depth - 1))
        h = h.reshape(bsz, seq, d)
    return h
```

```python
import functools

import numpy as np
import jax
import jax.numpy as jnp
from jax import lax
from jax.experimental import pallas as pl
from jax.experimental.pallas import tpu as pltpu

RMS_EPS = 1e-5
FOURIER_GROUPS = 8

V7X_F32_SUBLANES = 8
V7X_BF16_SUBLANES = 16
V7X_MXU_DIM = 256
V7X_VMEM_LIMIT_BYTES = 56 * 1024 * 1024

_BF16 = jnp.bfloat16
_F32 = jnp.float32


def _dot(a, b):
    return jnp.dot(a, b, preferred_element_type=_F32)


def _rmsnorm(x, g):
    r = lax.rsqrt(jnp.mean(x * x, axis=-1, keepdims=True) + RMS_EPS)
    return x * r * g


def _resident(shape):
    return pl.BlockSpec(shape, lambda *_: (0,) * len(shape), pipeline_mode=pl.Buffered(1))


def _params(*semantics):
    return pltpu.CompilerParams(dimension_semantics=semantics,
                                vmem_limit_bytes=V7X_VMEM_LIMIT_BYTES)


def _cast_specs(stacked, layer, steps, step_of):
    _, rows, cols = stacked.shape
    nblk = steps
    while rows % (nblk * V7X_BF16_SUBLANES):
        nblk //= 2
    rb, every = rows // nblk, steps // nblk
    in_spec = pl.BlockSpec((None, rb, cols), lambda *g: (layer, step_of(*g) // every, 0))
    out_spec = pl.BlockSpec((rb, cols), lambda *g: (step_of(*g) // every, 0))
    return in_spec, out_spec, jax.ShapeDtypeStruct((rows, cols), _BF16)


def _with_casts(body, n_in, n_out, n_cast):
    def kern(*refs):
        ins, rest = refs[:n_in], refs[n_in:]
        cast_in, rest = rest[:n_cast], rest[n_cast:]
        outs, rest = rest[:n_out], rest[n_out:]
        cast_out, scratch = rest[:n_cast], rest[n_cast:]
        for src, dst in zip(cast_in, cast_out):
            dst[...] = src[...].astype(_BF16)
        body(*ins, *outs, *scratch)
    return kern


def _conv_mixer_kernel(h_ref, hprev_ref, hnext_ref, g_ref, win_ref, k_ref, m_ref, *wbf,
                       tm, d, layer):
    halo = V7X_F32_SUBLANES
    i = pl.program_id(1)
    if wbf:
        @pl.when((pl.program_id(0) == 0) & (i == 0))
        def _():
            wbf[0][...] = win_ref[...].astype(_BF16)
        win_ref = wbf[0]
    g = g_ref[layer:layer + 1]
    k = k_ref[layer // 2]
    xn = jnp.concatenate([_rmsnorm(hnext_ref[...], g), _rmsnorm(hprev_ref[...], g),
                          _rmsnorm(h_ref[...], g)], axis=0).astype(_BF16)
    u = _dot(xn, win_ref[:, d:2 * d]) * _dot(xn, win_ref[:, 2 * d:3 * d])
    u_next = jnp.where(i == pl.num_programs(1) - 1, 0.0, u[0:halo])
    u_prev = jnp.where(i == 0, 0.0, u[halo:2 * halo])
    u_mid = u[2 * halo:]
    slab = jnp.concatenate([u_prev, u_mid, u_next], axis=0)
    up = pltpu.roll(slab, 1, 0)[halo:halo + tm]
    dn = pltpu.roll(slab, tm + 2 * halo - 1, 0)[halo:halo + tm]
    conv = k[0:1] * up + k[1:2] * u_mid + k[2:3] * dn
    b = _dot(xn[2 * halo:], win_ref[:, 0:d])
    m_ref[...] = (b * conv).astype(_BF16)


def _conv_mixer(h, g, w_in, k, casts, *, tm, layer):
    bsz, seq, d = h.shape
    halo = V7X_F32_SUBLANES
    nt = seq // tm
    hb = tm // halo
    last = seq // halo - 1
    cspecs = [_cast_specs(w, l, bsz * nt, lambda b, i: b * nt + i) for w, l in casts]
    body = functools.partial(_conv_mixer_kernel, tm=tm, d=d, layer=layer)
    if w_in.dtype == _F32:
        w_spec = pl.BlockSpec((None, d, 3 * d), lambda b, i: (0, 0, 0), pipeline_mode=pl.Buffered(1))
        scratch = [pltpu.VMEM((d, 3 * d), _BF16)]
    else:
        w_spec, scratch = _resident((d, 3 * d)), []
    m, *cast_out = pl.pallas_call(
        _with_casts(body, 6, 1, len(casts)),
        out_shape=[jax.ShapeDtypeStruct((bsz, seq, d), _BF16)] + [c[2] for c in cspecs],
        grid=(bsz, nt),
        in_specs=[
            pl.BlockSpec((None, tm, d), lambda b, i: (b, i, 0)),
            pl.BlockSpec((None, halo, d), lambda b, i: (b, jnp.maximum(i * hb - 1, 0), 0)),
            pl.BlockSpec((None, halo, d), lambda b, i: (b, jnp.minimum((i + 1) * hb, last), 0)),
            _resident(g.shape),
            w_spec,
            _resident(k.shape),
        ] + [c[0] for c in cspecs],
        out_specs=[pl.BlockSpec((None, tm, d), lambda b, i: (b, i, 0))] + [c[1] for c in cspecs],
        scratch_shapes=scratch,
        compiler_params=_params("arbitrary", "arbitrary"),
        name="conv_mixer",
    )(h, h, h, g, w_in, k, *[w for w, _ in casts])
    return m, cast_out


def _dft_tables(seq, gd, kb):
    n1 = V7X_MXU_DIM // V7X_F32_SUBLANES
    n2 = seq // n1
    n1c = n1 // 2 + 1
    grp = V7X_BF16_SUBLANES
    two_pi = 2.0 * np.pi
    th1 = two_pi * np.outer(np.arange(n1c), np.arange(n1)) / n1
    f1 = np.stack([np.cos(th1), -np.sin(th1)], axis=1).reshape(2 * n1c, n1) / np.sqrt(n1)
    l1 = np.kron(f1, np.eye(V7X_F32_SUBLANES))
    th2 = two_pi * np.outer(np.arange(n2), np.arange(n2)) / n2
    c2 = np.cos(th2) / np.sqrt(n2)
    s2 = np.sin(th2) / np.sqrt(n2)
    tw = (two_pi * np.outer(np.arange(n1c), np.arange(n2)) / seq).reshape(n1c, 1, n2)
    ph = two_pi * np.outer(np.arange(gd), np.arange(gd)) / gd
    cs = np.block([[np.cos(ph), np.cos(ph)], [np.sin(ph), -np.sin(ph)]]) / np.sqrt(gd)
    a, b = np.meshgrid(np.arange(grp), np.arange(grp), indexing="ij")
    perm = np.zeros((2, grp * grp, grp * grp))
    perm[0, (b * grp + a).ravel(), (a * grp + b).ravel()] = 1.0
    perm[1, (b * grp + a).ravel(), (a * grp + np.where(a == 0, b, grp - 1 - b)).ravel()] = 1.0
    f = lambda t: jnp.asarray(t, _F32)
    return n1, n2, n1c, f(l1), f(c2), f(s2), f(np.cos(tw)), f(np.sin(tw)), f(cs), f(perm)


def _fourier_stage1_kernel(h_ref, g_ref, l1_ref, t_ref, *, n1, n1c, jb, d, layer):
    g = g_ref[layer:layer + 1]
    l1 = l1_ref[...]
    sub = V7X_F32_SUBLANES
    for j in range(jb):
        halves = []
        for half in range(V7X_BF16_SUBLANES // sub):
            x = h_ref[:, j, half * sub:(half + 1) * sub, :].reshape(n1 * sub, d)
            xn = _rmsnorm(x, g).astype(_BF16)
            halves.append(_dot(l1, xn).reshape(2 * n1c, sub, d))
        t = jnp.concatenate(halves, axis=1).astype(_BF16)
        t_ref[:, :, j] = t.reshape(n1c, 2, V7X_BF16_SUBLANES, d)


def _fourier_stage1(h, g, l1, *, n1, n1c, n2, jb, layer):
    bsz, seq, d = h.shape
    grp = V7X_BF16_SUBLANES
    ng = n2 // grp
    kern = functools.partial(_fourier_stage1_kernel, n1=n1, n1c=n1c, jb=jb, d=d, layer=layer)
    return pl.pallas_call(
        kern,
        out_shape=jax.ShapeDtypeStruct((bsz, n1c, 2, ng, grp, d), _BF16),
        grid=(bsz, ng // jb),
        in_specs=[
            pl.BlockSpec((None, n1, jb, grp, d), lambda b, j: (b, 0, j, 0, 0)),
            _resident(g.shape),
            _resident(l1.shape),
        ],
        out_specs=pl.BlockSpec((None, n1c, 2, jb, grp, d), lambda b, j: (b, 0, 0, j, 0, 0)),
        compiler_params=_params("parallel", "parallel"),
        name="fourier_stage1",
    )(h.reshape(bsz, n1, ng, grp, d), g, l1)


def _fourier_stage2_kernel(t_ref, c2_ref, s2_ref, twc_ref, tws_ref, cs_ref, p_ref, y_ref,
                           ylo_scr, yhi_scr, *, kb, n2, d, gd):
    i = pl.program_id(1)
    grp = V7X_BF16_SUBLANES
    nmain = grp // kb

    def slab(kk):
        twc = twc_ref[kk]
        tws = tws_ref[kk]
        cth = (c2_ref[...] * twc - s2_ref[...] * tws).astype(_BF16)
        sth = s2_ref[...] * twc + c2_ref[...] * tws
        gmat = jnp.concatenate(
            [jnp.concatenate([cth, sth.astype(_BF16)], axis=1),
             jnp.concatenate([(-sth).astype(_BF16), cth], axis=1)], axis=0)
        u = _dot(gmat, t_ref[kk]).astype(_BF16)
        out = []
        for gi in range(d // gd):
            gsl = slice(gi * gd, (gi + 1) * gd)
            lhs = jnp.concatenate([u[0:n2, gsl], u[n2:2 * n2, gsl]], axis=1)
            out.append(_dot(lhs, cs_ref[...]))
        return out

    def interleave(src_scr, perm, mirrored):
        nblk = n2 // grp
        for k2h in range(nblk):
            pieces = []
            for k1l in range(grp):
                blk = nblk - 1 - k2h if (mirrored and k1l > 0) else k2h
                pieces.append(src_scr[k1l, blk * grp:(blk + 1) * grp, :])
            res = _dot(perm, jnp.concatenate(pieces, axis=0))
            y_ref[k2h * grp:(k2h + 1) * grp] = res.astype(_BF16).reshape(grp, grp, d)

    @pl.when(i < nmain)
    def _():
        for kk in range(kb):
            k1 = i * kb + kk
            mirror = (grp - k1) % grp
            for gi, pq in enumerate(slab(kk)):
                gsl = slice(gi * gd, (gi + 1) * gd)
                ylo_scr[k1, :, gsl] = pq[:, 0:gd].astype(_BF16)
                yhi_scr[mirror, :, gsl] = pq[:, gd:2 * gd].astype(_BF16)

    @pl.when(i == nmain - 1)
    def _():
        interleave(ylo_scr, p_ref[0], mirrored=False)

    @pl.when(i == nmain)
    def _():
        for gi, pq in enumerate(slab(0)):
            yhi_scr[0, :, gi * gd:(gi + 1) * gd] = pq[:, 0:gd].astype(_BF16)
        interleave(yhi_scr, p_ref[1], mirrored=True)


def _fourier_stage2(t1, c2, s2, twc, tws, cs, perm, *, n1, n1c, n2, d, kb):
    bsz = t1.shape[0]
    gd = d // FOURIER_GROUPS
    grp = V7X_BF16_SUBLANES
    nmain = grp // kb
    assert n1 == 2 * grp and n1c == nmain * kb + 1
    kern = functools.partial(_fourier_stage2_kernel, kb=kb, n2=n2, d=d, gd=gd)
    return pl.pallas_call(
        kern,
        out_shape=jax.ShapeDtypeStruct((bsz, n2, n1 // grp, grp, d), _BF16),
        grid=(bsz, nmain + 1),
        in_specs=[
            pl.BlockSpec((None, kb, 2 * n2, d), lambda b, i: (b, i, 0, 0)),
            _resident((n2, n2)),
            _resident((n2, n2)),
            pl.BlockSpec((kb, 1, n2), lambda b, i: (i, 0, 0)),
            pl.BlockSpec((kb, 1, n2), lambda b, i: (i, 0, 0)),
            _resident(cs.shape),
            _resident(perm.shape),
        ],
        out_specs=pl.BlockSpec((None, n2, None, grp, d), lambda b, i: (b, 0, i // nmain, 0, 0)),
        scratch_shapes=[
            pltpu.VMEM((grp, n2, d), _BF16),
            pltpu.VMEM((grp, n2, d), _BF16),
        ],
        compiler_params=_params("arbitrary", "arbitrary"),
        name="fourier_stage2",
    )(t1.reshape(bsz, n1c, 2 * n2, d), c2, s2, twc, tws, cs, perm)


def _ffn_kernel(h_ref, m_ref, gn_ref, wo_ref, wg_ref, wu_ref, wd_ref, gf_ref, o_ref,
                hn_scr, act_scr, *, dff, fchunk, nsub, layer, final):
    rs = h_ref.shape[0] // nsub
    for q in range(nsub):
        rows = slice(q * rs, (q + 1) * rs)
        h1 = h_ref[rows] + _dot(m_ref[rows], wo_ref[...])
        o_ref[rows] = h1
        hn_scr[rows] = _rmsnorm(h1, gn_ref[layer:layer + 1]).astype(_BF16)
    for q in range(nsub):
        rows = slice(q * rs, (q + 1) * rs)
        for c0 in range(0, dff, fchunk):
            a = _dot(hn_scr[rows], wg_ref[:, c0:c0 + fchunk])
            b = _dot(hn_scr[rows], wu_ref[:, c0:c0 + fchunk])
            act_scr[rows, c0:c0 + fchunk] = (a * jax.nn.sigmoid(a) * b).astype(_BF16)
        out = o_ref[rows] + _dot(act_scr[rows], wd_ref[...])
        if final:
            out = _rmsnorm(out, gf_ref[...])
        o_ref[rows] = out


def _ffn(h, m, gn, wo, wg, wu, wd, gf, *, tm, fchunk, nsub, layer, final):
    n, d = h.shape
    dff = wg.shape[1]
    kern = functools.partial(_ffn_kernel, dff=dff, fchunk=fchunk, nsub=nsub, layer=layer, final=final)
    row = lambda i: (i, 0)
    return pl.pallas_call(
        kern,
        out_shape=jax.ShapeDtypeStruct((n, d), _F32),
        grid=(n // tm,),
        in_specs=[
            pl.BlockSpec((tm, d), row),
            pl.BlockSpec((tm, d), row),
            _resident(gn.shape),
            _resident((d, d)),
            _resident((d, dff)),
            _resident((d, dff)),
            _resident((dff, d)),
            _resident((1, d)),
        ],
        out_specs=pl.BlockSpec((tm, d), row),
        scratch_shapes=[
            pltpu.VMEM((tm, d), _BF16),
            pltpu.VMEM((tm, dff), _BF16),
        ],
        compiler_params=_params("parallel"),
        name="outproj_swiglu",
    )(h, m, gn, wo, wg, wu, wd, gf)


def kernel(x, conv_w_in, conv_k, conv_w_out, fourier_w_out, mix_norm_g, ffn_norm_g,
           ffn_w_gate, ffn_w_up, ffn_w_down, final_norm_g):
    bsz, seq, d = x.shape
    depth = mix_norm_g.shape[0]
    dff = ffn_w_gate.shape[-1]

    tm_conv = 1024
    tm_ffn = 1024
    ffn_nsub = 2
    fchunk = 256
    assert seq % tm_conv == 0 and (bsz * seq) % tm_ffn == 0 and dff % fchunk == 0

    jb = 4
    kb = 8
    n1, n2, n1c, l1, c2, s2, twc, tws, cs, perm = _dft_tables(seq, d // FOURIER_GROUPS, kb)
    l1, cs, perm = l1.astype(_BF16), cs.astype(_BF16), perm.astype(_BF16)
    assert d % FOURIER_GROUPS == 0 and seq == n1 * n2
    assert n2 % (V7X_BF16_SUBLANES * jb) == 0 and n1 % V7X_BF16_SUBLANES == 0

    gfin = final_norm_g.reshape(1, d)
    is_conv = lambda i: i % 2 == 0

    def ffn_casts(i):
        wo = (conv_w_out, i // 2) if is_conv(i) else (fourier_w_out, i // 2)
        return [wo, (ffn_w_gate, i), (ffn_w_up, i), (ffn_w_down, i)]

    def mixer_casts(i):
        nxt = ffn_casts(i + 1) if i + 1 < depth else []
        w_in_next = [(conv_w_in, (i + 2) // 2)] if i + 2 < depth else []
        return ffn_casts(i) + nxt + w_in_next

    w_in = conv_w_in
    ffn_w = {}
    h = x
    for i in range(depth):
        if is_conv(i):
            m, cast = _conv_mixer(h, mix_norm_g, w_in, conv_k, mixer_casts(i), tm=tm_conv, layer=i)
            ffn_w[i], cast = cast[:4], cast[4:]
            if i + 1 < depth:
                ffn_w[i + 1], cast = cast[:4], cast[4:]
            if i + 2 < depth:
                w_in, = cast
        else:
            t1 = _fourier_stage1(h, mix_norm_g, l1, n1=n1, n1c=n1c, n2=n2, jb=jb, layer=i)
            m = _fourier_stage2(t1, c2, s2, twc, tws, cs, perm, n1=n1, n1c=n1c, n2=n2, d=d, kb=kb)
        h = _ffn(h.reshape(bsz * seq, d), m.reshape(bsz * seq, d), ffn_norm_g, *ffn_w[i], gfin,
                 tm=tm_ffn, fchunk=fchunk, nsub=ffn_nsub, layer=i, final=(i == depth - 1))
        h = h.reshape(bsz, seq, d)
    return h
```

```python
import functools

import numpy as np
import jax
import jax.numpy as jnp
from jax import lax
from jax.experimental import pallas as pl
from jax.experimental.pallas import tpu as pltpu

RMS_EPS = 1e-5
FOURIER_GROUPS = 8

V7X_F32_SUBLANES = 8
V7X_BF16_SUBLANES = 16
V7X_MXU_DIM = 256
V7X_VMEM_LIMIT_BYTES = 56 * 1024 * 1024

_BF16 = jnp.bfloat16
_F32 = jnp.float32


def _dot(a, b):
    return jnp.dot(a, b, preferred_element_type=_F32)


def _rmsnorm(x, g):
    r = lax.rsqrt(jnp.mean(x * x, axis=-1, keepdims=True) + RMS_EPS)
    return x * r * g


def _resident(shape):
    return pl.BlockSpec(shape, lambda *_: (0,) * len(shape), pipeline_mode=pl.Buffered(1))


def _params(*semantics):
    return pltpu.CompilerParams(dimension_semantics=semantics,
                                vmem_limit_bytes=V7X_VMEM_LIMIT_BYTES)


def _cast_specs(stacked, layer, steps, step_of):
    _, rows, cols = stacked.shape
    nblk = steps
    while rows % (nblk * V7X_BF16_SUBLANES):
        nblk //= 2
    rb, every = rows // nblk, steps // nblk
    in_spec = pl.BlockSpec((None, rb, cols), lambda *g: (layer, step_of(*g) // every, 0))
    out_spec = pl.BlockSpec((rb, cols), lambda *g: (step_of(*g) // every, 0))
    return in_spec, out_spec, jax.ShapeDtypeStruct((rows, cols), _BF16)


def _with_casts(body, n_in, n_out, n_cast):
    def kern(*refs):
        ins, rest = refs[:n_in], refs[n_in:]
        cast_in, rest = rest[:n_cast], rest[n_cast:]
        outs, rest = rest[:n_out], rest[n_out:]
        cast_out, scratch = rest[:n_cast], rest[n_cast:]
        for src, dst in zip(cast_in, cast_out):
            dst[...] = src[...].astype(_BF16)
        body(*ins, *outs, *scratch)
    return kern


def _conv_mixer_kernel(h_ref, hprev_ref, hnext_ref, g_ref, win_ref, k_ref, m_ref, *wbf,
                       tm, d, layer):
    halo = V7X_F32_SUBLANES
    i = pl.program_id(1)
    if wbf:
        @pl.when((pl.program_id(0) == 0) & (i == 0))
        def _():
            wbf[0][...] = win_ref[...].astype(_BF16)
        win_ref = wbf[0]
    g = g_ref[layer:layer + 1]
    k = k_ref[layer // 2]
    xn = jnp.concatenate([_rmsnorm(hnext_ref[...], g), _rmsnorm(hprev_ref[...], g),
                          _rmsnorm(h_ref[...], g)], axis=0).astype(_BF16)
    u = _dot(xn, win_ref[:, d:2 * d]) * _dot(xn, win_ref[:, 2 * d:3 * d])
    u_next = jnp.where(i == pl.num_programs(1) - 1, 0.0, u[0:halo])
    u_prev = jnp.where(i == 0, 0.0, u[halo:2 * halo])
    u_mid = u[2 * halo:]
    slab = jnp.concatenate([u_prev, u_mid, u_next], axis=0)
    up = pltpu.roll(slab, 1, 0)[halo:halo + tm]
    dn = pltpu.roll(slab, tm + 2 * halo - 1, 0)[halo:halo + tm]
    conv = k[0:1] * up + k[1:2] * u_mid + k[2:3] * dn
    b = _dot(xn[2 * halo:], win_ref[:, 0:d])
    m_ref[...] = (b * conv).astype(_BF16)


def _conv_mixer(h, g, w_in, k, casts, *, tm, layer):
    bsz, seq, d = h.shape
    halo = V7X_F32_SUBLANES
    nt = seq // tm
    hb = tm // halo
    last = seq // halo - 1
    cspecs = [_cast_specs(w, l, bsz * nt, lambda b, i: b * nt + i) for w, l in casts]
    body = functools.partial(_conv_mixer_kernel, tm=tm, d=d, layer=layer)
    if w_in.dtype == _F32:
        w_spec = pl.BlockSpec((None, d, 3 * d), lambda b, i: (0, 0, 0), pipeline_mode=pl.Buffered(1))
        scratch = [pltpu.VMEM((d, 3 * d), _BF16)]
    else:
        w_spec, scratch = _resident((d, 3 * d)), []
    m, *cast_out = pl.pallas_call(
        _with_casts(body, 6, 1, len(casts)),
        out_shape=[jax.ShapeDtypeStruct((bsz, seq, d), _BF16)] + [c[2] for c in cspecs],
        grid=(bsz, nt),
        in_specs=[
            pl.BlockSpec((None, tm, d), lambda b, i: (b, i, 0)),
            pl.BlockSpec((None, halo, d), lambda b, i: (b, jnp.maximum(i * hb - 1, 0), 0)),
            pl.BlockSpec((None, halo, d), lambda b, i: (b, jnp.minimum((i + 1) * hb, last), 0)),
            _resident(g.shape),
            w_spec,
            _resident(k.shape),
        ] + [c[0] for c in cspecs],
        out_specs=[pl.BlockSpec((None, tm, d), lambda b, i: (b, i, 0))] + [c[1] for c in cspecs],
        scratch_shapes=scratch,
        compiler_params=_params("arbitrary", "arbitrary"),
        name="conv_mixer",
    )(h, h, h, g, w_in, k, *[w for w, _ in casts])
    return m, cast_out


def _dft_tables(seq, gd, kb):
    n1 = V7X_MXU_DIM // V7X_F32_SUBLANES
    n2 = seq // n1
    n1c = n1 // 2 + 1
    grp = V7X_BF16_SUBLANES
    two_pi = 2.0 * np.pi
    th1 = two_pi * np.outer(np.arange(n1c), np.arange(n1)) / n1
    f1 = np.stack([np.cos(th1), -np.sin(th1)], axis=1).reshape(2 * n1c, n1) / np.sqrt(n1)
    l1 = np.kron(f1, np.eye(V7X_F32_SUBLANES))
    th2 = two_pi * np.outer(np.arange(n2), np.arange(n2)) / n2
    c2 = np.cos(th2) / np.sqrt(n2)
    s2 = np.sin(th2) / np.sqrt(n2)
    tw = (two_pi * np.outer(np.arange(n1c), np.arange(n2)) / seq).reshape(n1c, 1, n2)
    ph = two_pi * np.outer(np.arange(gd), np.arange(gd)) / gd
    cs = np.block([[np.cos(ph), np.cos(ph)], [np.sin(ph), -np.sin(ph)]]) / np.sqrt(gd)
    a, b = np.meshgrid(np.arange(grp), np.arange(grp), indexing="ij")
    perm = np.zeros((2, grp * grp, grp * grp))
    perm[0, (b * grp + a).ravel(), (a * grp + b).ravel()] = 1.0
    perm[1, (b * grp + a).ravel(), (a * grp + np.where(a == 0, b, grp - 1 - b)).ravel()] = 1.0
    f = lambda t: jnp.asarray(t, _F32)
    return n1, n2, n1c, f(l1), f(c2), f(s2), f(np.cos(tw)), f(np.sin(tw)), f(cs), f(perm)


def _fourier_stage1_kernel(h_ref, g_ref, l1_ref, t_ref, *, n1, n1c, jb, d, layer):
    g = g_ref[layer:layer + 1]
    l1 = l1_ref[...]
    sub = V7X_F32_SUBLANES
    for j in range(jb):
        halves = []
        for half in range(V7X_BF16_SUBLANES // sub):
            x = h_ref[:, j, half * sub:(half + 1) * sub, :].reshape(n1 * sub, d)
            xn = _rmsnorm(x, g).astype(_BF16)
            halves.append(_dot(l1, xn).reshape(2 * n1c, sub, d))
        t = jnp.concatenate(halves, axis=1).astype(_BF16)
        t_ref[:, :, j] = t.reshape(n1c, 2, V7X_BF16_SUBLANES, d)


def _fourier_stage1(h, g, l1, *, n1, n1c, n2, jb, layer):
    bsz, seq, d = h.shape
    grp = V7X_BF16_SUBLANES
    ng = n2 // grp
    kern = functools.partial(_fourier_stage1_kernel, n1=n1, n1c=n1c, jb=jb, d=d, layer=layer)
    return pl.pallas_call(
        kern,
        out_shape=jax.ShapeDtypeStruct((bsz, n1c, 2, ng, grp, d), _BF16),
        grid=(bsz, ng // jb),
        in_specs=[
            pl.BlockSpec((None, n1, jb, grp, d), lambda b, j: (b, 0, j, 0, 0)),
            _resident(g.shape),
            _resident(l1.shape),
        ],
        out_specs=pl.BlockSpec((None, n1c, 2, jb, grp, d), lambda b, j: (b, 0, 0, j, 0, 0)),
        compiler_params=_params("parallel", "parallel"),
        name="fourier_stage1",
    )(h.reshape(bsz, n1, ng, grp, d), g, l1)


def _fourier_stage2_kernel(t_ref, c2_ref, s2_ref, twc_ref, tws_ref, cs_ref, p_ref, y_ref,
                           ylo_scr, yhi_scr, *, kb, n2, d, gd):
    i = pl.program_id(1)
    grp = V7X_BF16_SUBLANES
    nmain = grp // kb

    def slab(kk):
        twc = twc_ref[kk]
        tws = tws_ref[kk]
        cth = (c2_ref[...] * twc - s2_ref[...] * tws).astype(_BF16)
        sth = s2_ref[...] * twc + c2_ref[...] * tws
        gmat = jnp.concatenate(
            [jnp.concatenate([cth, sth.astype(_BF16)], axis=1),
             jnp.concatenate([(-sth).astype(_BF16), cth], axis=1)], axis=0)
        u = _dot(gmat, t_ref[kk]).astype(_BF16)
        out = []
        for gi in range(d // gd):
            gsl = slice(gi * gd, (gi + 1) * gd)
            lhs = jnp.concatenate([u[0:n2, gsl], u[n2:2 * n2, gsl]], axis=1)
            out.append(_dot(lhs, cs_ref[...]))
        return out

    def interleave(src_scr, perm, mirrored):
        nblk = n2 // grp
        for k2h in range(nblk):
            pieces = []
            for k1l in range(grp):
                blk = nblk - 1 - k2h if (mirrored and k1l > 0) else k2h
                pieces.append(src_scr[k1l, blk * grp:(blk + 1) * grp, :])
            res = _dot(perm, jnp.concatenate(pieces, axis=0))
            y_ref[k2h * grp:(k2h + 1) * grp] = res.astype(_BF16).reshape(grp, grp, d)

    @pl.when(i < nmain)
    def _():
        for kk in range(kb):
            k1 = i * kb + kk
            mirror = (grp - k1) % grp
            for gi, pq in enumerate(slab(kk)):
                gsl = slice(gi * gd, (gi + 1) * gd)
                ylo_scr[k1, :, gsl] = pq[:, 0:gd].astype(_BF16)
                yhi_scr[mirror, :, gsl] = pq[:, gd:2 * gd].astype(_BF16)

    @pl.when(i == nmain - 1)
    def _():
        interleave(ylo_scr, p_ref[0], mirrored=False)

    @pl.when(i == nmain)
    def _():
        for gi, pq in enumerate(slab(0)):
            yhi_scr[0, :, gi * gd:(gi + 1) * gd] = pq[:, 0:gd].astype(_BF16)
        interleave(yhi_scr, p_ref[1], mirrored=True)


def _fourier_stage2(t1, c2, s2, twc, tws, cs, perm, *, n1, n1c, n2, d, kb):
    bsz = t1.shape[0]
    gd = d // FOURIER_GROUPS
    grp = V7X_BF16_SUBLANES
    nmain = grp // kb
    assert n1 == 2 * grp and n1c == nmain * kb + 1
    kern = functools.partial(_fourier_stage2_kernel, kb=kb, n2=n2, d=d, gd=gd)
    return pl.pallas_call(
        kern,
        out_shape=jax.ShapeDtypeStruct((bsz, n2, n1 // grp, grp, d), _BF16),
        grid=(bsz, nmain + 1),
        in_specs=[
            pl.BlockSpec((None, kb, 2 * n2, d), lambda b, i: (b, i, 0, 0)),
            _resident((n2, n2)),
            _resident((n2, n2)),
            pl.BlockSpec((kb, 1, n2), lambda b, i: (i, 0, 0)),
            pl.BlockSpec((kb, 1, n2), lambda b, i: (i, 0, 0)),
            _resident(cs.shape),
            _resident(perm.shape),
        ],
        out_specs=pl.BlockSpec((None, n2, None, grp, d), lambda b, i: (b, 0, i // nmain, 0, 0)),
        scratch_shapes=[
            pltpu.VMEM((grp, n2, d), _BF16),
            pltpu.VMEM((grp, n2, d), _BF16),
        ],
        compiler_params=_params("arbitrary", "arbitrary"),
        name="fourier_stage2",
    )(t1.reshape(bsz, n1c, 2 * n2, d), c2, s2, twc, tws, cs, perm)


def _ffn_kernel(h_ref, m_ref, gn_ref, wo_hbm, wg_hbm, wu_hbm, wd_hbm, gf_ref, o_ref,
                wo_ref, wg_ref, wu_ref, wd_ref, sems, hn_scr, act_scr,
                *, dff, fchunk, nsub, ngroup, layer, final):
    gcols = -(-(dff // fchunk) // ngroup) * fchunk
    groups = [slice(g0, min(g0 + gcols, dff)) for g0 in range(0, dff, gcols)]
    copy_wo = pltpu.make_async_copy(wo_hbm, wo_ref, sems.at[0])
    copy_wd = pltpu.make_async_copy(wd_hbm, wd_ref, sems.at[1])
    copy_gu = [(pltpu.make_async_copy(wg_hbm.at[:, cols], wg_ref.at[:, cols], sems.at[2 + 2 * g]),
                pltpu.make_async_copy(wu_hbm.at[:, cols], wu_ref.at[:, cols], sems.at[3 + 2 * g]))
               for g, cols in enumerate(groups)]

    def body(first_step):
        if first_step:
            copy_wo.start()
            for cg, cu in copy_gu:
                cg.start()
                cu.start()
            copy_wd.start()
            copy_wo.wait()
        rs = h_ref.shape[0] // nsub
        for q in range(nsub):
            rows = slice(q * rs, (q + 1) * rs)
            h1 = h_ref[rows] + _dot(m_ref[rows], wo_ref[...])
            o_ref[rows] = h1
            hn_scr[rows] = _rmsnorm(h1, gn_ref[layer:layer + 1]).astype(_BF16)
        for q in range(nsub):
            rows = slice(q * rs, (q + 1) * rs)
            for c0 in range(0, dff, fchunk):
                if first_step and q == 0 and c0 % gcols == 0:
                    cg, cu = copy_gu[c0 // gcols]
                    cg.wait()
                    cu.wait()
                a = _dot(hn_scr[rows], wg_ref[:, c0:c0 + fchunk])
                b = _dot(hn_scr[rows], wu_ref[:, c0:c0 + fchunk])
                act_scr[rows, c0:c0 + fchunk] = (a * jax.nn.sigmoid(a) * b).astype(_BF16)
            if first_step and q == 0:
                copy_wd.wait()
            out = o_ref[rows] + _dot(act_scr[rows], wd_ref[...])
            if final:
                out = _rmsnorm(out, gf_ref[...])
            o_ref[rows] = out

    @pl.when(pl.program_id(0) == 0)
    def _():
        body(True)

    @pl.when(pl.program_id(0) > 0)
    def _():
        body(False)


def _ffn(h, m, gn, wo, wg, wu, wd, gf, *, tm, fchunk, nsub, layer, final):
    n, d = h.shape
    dff = wg.shape[1]
    ngroup = 3
    kern = functools.partial(_ffn_kernel, dff=dff, fchunk=fchunk, nsub=nsub, ngroup=ngroup,
                             layer=layer, final=final)
    row = lambda i: (i, 0)
    hbm = pl.BlockSpec(memory_space=pl.ANY)
    return pl.pallas_call(
        kern,
        out_shape=jax.ShapeDtypeStruct((n, d), _F32),
        grid=(n // tm,),
        in_specs=[
            pl.BlockSpec((tm, d), row),
            pl.BlockSpec((tm, d), row),
            _resident(gn.shape),
            hbm, hbm, hbm, hbm,
            _resident((1, d)),
        ],
        out_specs=pl.BlockSpec((tm, d), row),
        scratch_shapes=[
            pltpu.VMEM((d, d), _BF16),
            pltpu.VMEM((d, dff), _BF16),
            pltpu.VMEM((d, dff), _BF16),
            pltpu.VMEM((dff, d), _BF16),
            pltpu.SemaphoreType.DMA((2 + 2 * ngroup,)),
            pltpu.VMEM((tm, d), _BF16),
            pltpu.VMEM((tm, dff), _BF16),
        ],
        compiler_params=_params("arbitrary"),
        name="outproj_swiglu",
    )(h, m, gn, wo, wg, wu, wd, gf)


def kernel(x, conv_w_in, conv_k, conv_w_out, fourier_w_out, mix_norm_g, ffn_norm_g,
           ffn_w_gate, ffn_w_up, ffn_w_down, final_norm_g):
    bsz, seq, d = x.shape
    depth = mix_norm_g.shape[0]
    dff = ffn_w_gate.shape[-1]

    tm_conv = 1024
    tm_ffn = 1024
    ffn_nsub = 2
    fchunk = 256
    assert seq % tm_conv == 0 and (bsz * seq) % tm_ffn == 0 and dff % fchunk == 0

    jb = 4
    kb = 4
    n1, n2, n1c, l1, c2, s2, twc, tws, cs, perm = _dft_tables(seq, d // FOURIER_GROUPS, kb)
    l1, cs, perm = l1.astype(_BF16), cs.astype(_BF16), perm.astype(_BF16)
    assert d % FOURIER_GROUPS == 0 and seq == n1 * n2
    assert n2 % (V7X_BF16_SUBLANES * jb) == 0 and n1 % V7X_BF16_SUBLANES == 0

    gfin = final_norm_g.reshape(1, d)
    is_conv = lambda i: i % 2 == 0

    def ffn_casts(i):
        wo = (conv_w_out, i // 2) if is_conv(i) else (fourier_w_out, i // 2)
        return [wo, (ffn_w_gate, i), (ffn_w_up, i), (ffn_w_down, i)]

    def mixer_casts(i):
        nxt = ffn_casts(i + 1) if i + 1 < depth else []
        w_in_next = [(conv_w_in, (i + 2) // 2)] if i + 2 < depth else []
        return ffn_casts(i) + nxt + w_in_next

    w_in = conv_w_in
    ffn_w = {}
    h = x
    for i in range(depth):
        if is_conv(i):
            m, cast = _conv_mixer(h, mix_norm_g, w_in, conv_k, mixer_casts(i), tm=tm_conv, layer=i)
            ffn_w[i], cast = cast[:4], cast[4:]
            if i + 1 < depth:
                ffn_w[i + 1], cast = cast[:4], cast[4:]
            if i + 2 < depth:
                w_in, = cast
        else:
            t1 = _fourier_stage1(h, mix_norm_g, l1, n1=n1, n1c=n1c, n2=n2, jb=jb, layer=i)
            m = _fourier_stage2(t1, c2, s2, twc, tws, cs, perm, n1=n1, n1c=n1c, n2=n2, d=d, kb=kb)
        h = _ffn(h.reshape(bsz * seq, d), m.reshape(bsz * seq, d), ffn_norm_g, *ffn_w[i], gfin,
                 tm=tm_ffn, fchunk=fchunk, nsub=ffn_nsub, layer=i, final=(i == depth - 1))
        h = h.reshape(bsz, seq, d)
    return h
```

```python
import functools

import numpy as np
import jax
import jax.numpy as jnp
from jax import lax
from jax.experimental import pallas as pl
from jax.experimental.pallas import tpu as pltpu

RMS_EPS = 1e-5
FOURIER_GROUPS = 8

V7X_F32_SUBLANES = 8
V7X_BF16_SUBLANES = 16
V7X_MXU_DIM = 256
V7X_VMEM_LIMIT_BYTES = 56 * 1024 * 1024

_BF16 = jnp.bfloat16
_F32 = jnp.float32


def _dot(a, b):
    return jnp.dot(a, b, preferred_element_type=_F32)


def _rmsnorm(x, g):
    r = lax.rsqrt(jnp.mean(x * x, axis=-1, keepdims=True) + RMS_EPS)
    return x * r * g


def _resident(shape):
    return pl.BlockSpec(shape, lambda *_: (0,) * len(shape), pipeline_mode=pl.Buffered(1))


def _params(*semantics):
    return pltpu.CompilerParams(dimension_semantics=semantics,
                                vmem_limit_bytes=V7X_VMEM_LIMIT_BYTES)


def _cast_specs(stacked, layer, steps, step_of):
    _, rows, cols = stacked.shape
    nblk = steps
    while rows % (nblk * V7X_BF16_SUBLANES):
        nblk //= 2
    rb, every = rows // nblk, steps // nblk
    in_spec = pl.BlockSpec((None, rb, cols), lambda *g: (layer, step_of(*g) // every, 0))
    out_spec = pl.BlockSpec((rb, cols), lambda *g: (step_of(*g) // every, 0))
    return in_spec, out_spec, jax.ShapeDtypeStruct((rows, cols), _BF16)


def _with_casts(body, n_in, n_out, n_cast):
    def kern(*refs):
        ins, rest = refs[:n_in], refs[n_in:]
        cast_in, rest = rest[:n_cast], rest[n_cast:]
        outs, rest = rest[:n_out], rest[n_out:]
        cast_out, scratch = rest[:n_cast], rest[n_cast:]
        for src, dst in zip(cast_in, cast_out):
            dst[...] = src[...].astype(_BF16)
        body(*ins, *outs, *scratch)
    return kern


def _conv_mixer_kernel(h_ref, hprev_ref, hnext_ref, g_ref, win_ref, k_ref, m_ref, *wbf,
                       tm, d, layer):
    halo = V7X_F32_SUBLANES
    i = pl.program_id(1)
    if wbf:
        @pl.when((pl.program_id(0) == 0) & (i == 0))
        def _():
            wbf[0][...] = win_ref[...].astype(_BF16)
        win_ref = wbf[0]
    g = g_ref[layer:layer + 1]
    k = k_ref[layer // 2]
    xn = jnp.concatenate([_rmsnorm(hnext_ref[...], g), _rmsnorm(hprev_ref[...], g),
                          _rmsnorm(h_ref[...], g)], axis=0).astype(_BF16)
    u = _dot(xn, win_ref[:, d:2 * d]) * _dot(xn, win_ref[:, 2 * d:3 * d])
    u_next = jnp.where(i == pl.num_programs(1) - 1, 0.0, u[0:halo])
    u_prev = jnp.where(i == 0, 0.0, u[halo:2 * halo])
    u_mid = u[2 * halo:]
    slab = jnp.concatenate([u_prev, u_mid, u_next], axis=0)
    up = pltpu.roll(slab, 1, 0)[halo:halo + tm]
    dn = pltpu.roll(slab, tm + 2 * halo - 1, 0)[halo:halo + tm]
    conv = k[0:1] * up + k[1:2] * u_mid + k[2:3] * dn
    b = _dot(xn[2 * halo:], win_ref[:, 0:d])
    m_ref[...] = (b * conv).astype(_BF16)


def _conv_mixer(h, g, w_in, k, casts, *, tm, layer):
    bsz, seq, d = h.shape
    halo = V7X_F32_SUBLANES
    nt = seq // tm
    hb = tm // halo
    last = seq // halo - 1
    cspecs = [_cast_specs(w, l, bsz * nt, lambda b, i: b * nt + i) for w, l in casts]
    body = functools.partial(_conv_mixer_kernel, tm=tm, d=d, layer=layer)
    if w_in.dtype == _F32:
        w_spec = pl.BlockSpec((None, d, 3 * d), lambda b, i: (0, 0, 0), pipeline_mode=pl.Buffered(1))
        scratch = [pltpu.VMEM((d, 3 * d), _BF16)]
    else:
        w_spec, scratch = _resident((d, 3 * d)), []
    m, *cast_out = pl.pallas_call(
        _with_casts(body, 6, 1, len(casts)),
        out_shape=[jax.ShapeDtypeStruct((bsz, seq, d), _BF16)] + [c[2] for c in cspecs],
        grid=(bsz, nt),
        in_specs=[
            pl.BlockSpec((None, tm, d), lambda b, i: (b, i, 0)),
            pl.BlockSpec((None, halo, d), lambda b, i: (b, jnp.maximum(i * hb - 1, 0), 0)),
            pl.BlockSpec((None, halo, d), lambda b, i: (b, jnp.minimum((i + 1) * hb, last), 0)),
            _resident(g.shape),
            w_spec,
            _resident(k.shape),
        ] + [c[0] for c in cspecs],
        out_specs=[pl.BlockSpec((None, tm, d), lambda b, i: (b, i, 0))] + [c[1] for c in cspecs],
        scratch_shapes=scratch,
        compiler_params=_params("arbitrary", "arbitrary"),
        name="conv_mixer",
    )(h, h, h, g, w_in, k, *[w for w, _ in casts])
    return m, cast_out


def _dft_tables(seq, gd, kb):
    n1 = V7X_MXU_DIM // V7X_F32_SUBLANES
    n2 = seq // n1
    n1c = n1 // 2 + 1
    grp = V7X_BF16_SUBLANES
    two_pi = 2.0 * np.pi
    th1 = two_pi * np.outer(np.arange(n1c), np.arange(n1)) / n1
    f1 = np.stack([np.cos(th1), -np.sin(th1)], axis=1).reshape(2 * n1c, n1) / np.sqrt(n1)
    l1 = np.kron(f1, np.eye(V7X_F32_SUBLANES))
    th2 = two_pi * np.outer(np.arange(n2), np.arange(n2)) / n2
    c2 = np.cos(th2) / np.sqrt(n2)
    s2 = np.sin(th2) / np.sqrt(n2)
    tw = (two_pi * np.outer(np.arange(n1c), np.arange(n2)) / seq).reshape(n1c, 1, n2)
    ph = two_pi * np.outer(np.arange(gd), np.arange(gd)) / gd
    cs = np.block([[np.cos(ph), np.cos(ph)], [np.sin(ph), -np.sin(ph)]]) / np.sqrt(gd)
    a, b = np.meshgrid(np.arange(grp), np.arange(grp), indexing="ij")
    perm = np.zeros((2, grp * grp, grp * grp))
    perm[0, (b * grp + a).ravel(), (a * grp + b).ravel()] = 1.0
    perm[1, (b * grp + a).ravel(), (a * grp + np.where(a == 0, b, grp - 1 - b)).ravel()] = 1.0
    f = lambda t: jnp.asarray(t, _F32)
    return n1, n2, n1c, f(l1), f(c2), f(s2), f(np.cos(tw)), f(np.sin(tw)), f(cs), f(perm)


def _fourier_stage1_kernel(h_ref, g_ref, l1_ref, t_ref, *, n1, n1c, jb, d, layer):
    g = g_ref[layer:layer + 1]
    l1 = l1_ref[...]
    sub = V7X_F32_SUBLANES
    for j in range(jb):
        halves = []
        for half in range(V7X_BF16_SUBLANES // sub):
            x = h_ref[:, j, half * sub:(half + 1) * sub, :].reshape(n1 * sub, d)
            xn = _rmsnorm(x, g).astype(_BF16)
            halves.append(_dot(l1, xn).reshape(2 * n1c, sub, d))
        t = jnp.concatenate(halves, axis=1).astype(_BF16)
        t_ref[:, :, j] = t.reshape(n1c, 2, V7X_BF16_SUBLANES, d)


def _fourier_stage1(h, g, l1, *, n1, n1c, n2, jb, layer):
    bsz, seq, d = h.shape
    grp = V7X_BF16_SUBLANES
    ng = n2 // grp
    kern = functools.partial(_fourier_stage1_kernel, n1=n1, n1c=n1c, jb=jb, d=d, layer=layer)
    return pl.pallas_call(
        kern,
        out_shape=jax.ShapeDtypeStruct((bsz, n1c, 2, ng, grp, d), _BF16),
        grid=(bsz, ng // jb),
        in_specs=[
            pl.BlockSpec((None, n1, jb, grp, d), lambda b, j: (b, 0, j, 0, 0)),
            _resident(g.shape),
            _resident(l1.shape),
        ],
        out_specs=pl.BlockSpec((None, n1c, 2, jb, grp, d), lambda b, j: (b, 0, 0, j, 0, 0)),
        compiler_params=_params("parallel", "parallel"),
        name="fourier_stage1",
    )(h.reshape(bsz, n1, ng, grp, d), g, l1)


def _fourier_stage2_kernel(t_ref, c2_ref, s2_ref, twc_ref, tws_ref, cs_ref, p_ref, y_ref,
                           ylo_scr, yhi_scr, *, kb, n2, d, gd):
    i = pl.program_id(1)
    grp = V7X_BF16_SUBLANES
    nmain = grp // kb

    def slab(kk):
        twc = twc_ref[kk]
        tws = tws_ref[kk]
        cth = (c2_ref[...] * twc - s2_ref[...] * tws).astype(_BF16)
        sth = s2_ref[...] * twc + c2_ref[...] * tws
        gmat = jnp.concatenate(
            [jnp.concatenate([cth, sth.astype(_BF16)], axis=1),
             jnp.concatenate([(-sth).astype(_BF16), cth], axis=1)], axis=0)
        u = _dot(gmat, t_ref[kk]).astype(_BF16)
        out = []
        for gi in range(d // gd):
            gsl = slice(gi * gd, (gi + 1) * gd)
            lhs = jnp.concatenate([u[0:n2, gsl], u[n2:2 * n2, gsl]], axis=1)
            out.append(_dot(lhs, cs_ref[...]))
        return out

    def interleave(src_scr, perm, mirrored):
        nblk = n2 // grp
        for k2h in range(nblk):
            pieces = []
            for k1l in range(grp):
                blk = nblk - 1 - k2h if (mirrored and k1l > 0) else k2h
                pieces.append(src_scr[k1l, blk * grp:(blk + 1) * grp, :])
            res = _dot(perm, jnp.concatenate(pieces, axis=0))
            y_ref[k2h * grp:(k2h + 1) * grp] = res.astype(_BF16).reshape(grp, grp, d)

    @pl.when(i < nmain)
    def _():
        for kk in range(kb):
            k1 = i * kb + kk
            mirror = (grp - k1) % grp
            for gi, pq in enumerate(slab(kk)):
                gsl = slice(gi * gd, (gi + 1) * gd)
                ylo_scr[k1, :, gsl] = pq[:, 0:gd].astype(_BF16)
                yhi_scr[mirror, :, gsl] = pq[:, gd:2 * gd].astype(_BF16)

    @pl.when(i == nmain - 1)
    def _():
        interleave(ylo_scr, p_ref[0], mirrored=False)

    @pl.when(i == nmain)
    def _():
        for gi, pq in enumerate(slab(0)):
            yhi_scr[0, :, gi * gd:(gi + 1) * gd] = pq[:, 0:gd].astype(_BF16)
        interleave(yhi_scr, p_ref[1], mirrored=True)


def _fourier_stage2(t1, c2, s2, twc, tws, cs, perm, *, n1, n1c, n2, d, kb):
    bsz = t1.shape[0]
    gd = d // FOURIER_GROUPS
    grp = V7X_BF16_SUBLANES
    nmain = grp // kb
    assert n1 == 2 * grp and n1c == nmain * kb + 1
    kern = functools.partial(_fourier_stage2_kernel, kb=kb, n2=n2, d=d, gd=gd)
    return pl.pallas_call(
        kern,
        out_shape=jax.ShapeDtypeStruct((bsz, n2, n1 // grp, grp, d), _BF16),
        grid=(bsz, nmain + 1),
        in_specs=[
            pl.BlockSpec((None, kb, 2 * n2, d), lambda b, i: (b, i, 0, 0)),
            _resident((n2, n2)),
            _resident((n2, n2)),
            pl.BlockSpec((kb, 1, n2), lambda b, i: (i, 0, 0)),
            pl.BlockSpec((kb, 1, n2), lambda b, i: (i, 0, 0)),
            _resident(cs.shape),
            _resident(perm.shape),
        ],
        out_specs=pl.BlockSpec((None, n2, None, grp, d), lambda b, i: (b, 0, i // nmain, 0, 0)),
        scratch_shapes=[
            pltpu.VMEM((grp, n2, d), _BF16),
            pltpu.VMEM((grp, n2, d), _BF16),
        ],
        compiler_params=_params("arbitrary", "arbitrary"),
        name="fourier_stage2",
    )(t1.reshape(bsz, n1c, 2 * n2, d), c2, s2, twc, tws, cs, perm)


def _ffn_kernel(h_ref, m_ref, gn_ref, wo_hbm, wg_hbm, wu_hbm, wd_hbm, gf_ref, o_ref,
                wo_ref, wg_ref, wu_ref, wd_ref, sems, hn_scr, act_scr,
                *, dff, fchunk, nsub, ngroup, layer, final):
    gcols = -(-(dff // fchunk) // ngroup) * fchunk
    groups = [slice(g0, min(g0 + gcols, dff)) for g0 in range(0, dff, gcols)]
    copy_wo = pltpu.make_async_copy(wo_hbm, wo_ref, sems.at[0])
    copy_wd = pltpu.make_async_copy(wd_hbm, wd_ref, sems.at[1])
    copy_gu = [(pltpu.make_async_copy(wg_hbm.at[:, cols], wg_ref.at[:, cols], sems.at[2 + 2 * g]),
                pltpu.make_async_copy(wu_hbm.at[:, cols], wu_ref.at[:, cols], sems.at[3 + 2 * g]))
               for g, cols in enumerate(groups)]

    def body(first_step):
        pending = [c for pair in copy_gu for c in pair] + [copy_wd]

        def start_next(n):
            for c in pending[:n]:
                c.start()
            del pending[:n]

        if first_step:
            copy_wo.start()
            start_next(2)
            copy_wo.wait()
            start_next(2)
        rs = h_ref.shape[0] // nsub
        for q in range(nsub):
            rows = slice(q * rs, (q + 1) * rs)
            h1 = h_ref[rows] + _dot(m_ref[rows], wo_ref[...])
            o_ref[rows] = h1
            hn_scr[rows] = _rmsnorm(h1, gn_ref[layer:layer + 1]).astype(_BF16)
        for q in range(nsub):
            rows = slice(q * rs, (q + 1) * rs)
            for c0 in range(0, dff, fchunk):
                if first_step and q == 0 and c0 % gcols == 0:
                    cg, cu = copy_gu[c0 // gcols]
                    cg.wait()
                    cu.wait()
                    start_next(2)
                a = _dot(hn_scr[rows], wg_ref[:, c0:c0 + fchunk])
                b = _dot(hn_scr[rows], wu_ref[:, c0:c0 + fchunk])
                act_scr[rows, c0:c0 + fchunk] = (a * jax.nn.sigmoid(a) * b).astype(_BF16)
            if first_step and q == 0:
                copy_wd.wait()
            out = o_ref[rows] + _dot(act_scr[rows], wd_ref[...])
            if final:
                out = _rmsnorm(out, gf_ref[...])
            o_ref[rows] = out

    @pl.when(pl.program_id(0) == 0)
    def _():
        body(True)

    @pl.when(pl.program_id(0) > 0)
    def _():
        body(False)


def _ffn(h, m, gn, wo, wg, wu, wd, gf, *, tm, fchunk, nsub, layer, final):
    n, d = h.shape
    dff = wg.shape[1]
    ngroup = 3
    kern = functools.partial(_ffn_kernel, dff=dff, fchunk=fchunk, nsub=nsub, ngroup=ngroup,
                             layer=layer, final=final)
    row = lambda i: (i, 0)
    hbm = pl.BlockSpec(memory_space=pl.ANY)
    return pl.pallas_call(
        kern,
        out_shape=jax.ShapeDtypeStruct((n, d), _F32),
        grid=(n // tm,),
        in_specs=[
            pl.BlockSpec((tm, d), row),
            pl.BlockSpec((tm, d), row),
            _resident(gn.shape),
            hbm, hbm, hbm, hbm,
            _resident((1, d)),
        ],
        out_specs=pl.BlockSpec((tm, d), row),
        scratch_shapes=[
            pltpu.VMEM((d, d), _BF16),
            pltpu.VMEM((d, dff), _BF16),
            pltpu.VMEM((d, dff), _BF16),
            pltpu.VMEM((dff, d), _BF16),
            pltpu.SemaphoreType.DMA((2 + 2 * ngroup,)),
            pltpu.VMEM((tm, d), _BF16),
            pltpu.VMEM((tm, dff), _BF16),
        ],
        compiler_params=_params("arbitrary"),
        name="outproj_swiglu",
    )(h, m, gn, wo, wg, wu, wd, gf)


def kernel(x, conv_w_in, conv_k, conv_w_out, fourier_w_out, mix_norm_g, ffn_norm_g,
           ffn_w_gate, ffn_w_up, ffn_w_down, final_norm_g):
    bsz, seq, d = x.shape
    depth = mix_norm_g.shape[0]
    dff = ffn_w_gate.shape[-1]

    tm_conv = 1024
    tm_ffn = 1024
    ffn_nsub = 2
    fchunk = 256
    assert seq % tm_conv == 0 and (bsz * seq) % tm_ffn == 0 and dff % fchunk == 0

    jb = 4
    kb = 4
    n1, n2, n1c, l1, c2, s2, twc, tws, cs, perm = _dft_tables(seq, d // FOURIER_GROUPS, kb)
    l1, cs, perm = l1.astype(_BF16), cs.astype(_BF16), perm.astype(_BF16)
    assert d % FOURIER_GROUPS == 0 and seq == n1 * n2
    assert n2 % (V7X_BF16_SUBLANES * jb) == 0 and n1 % V7X_BF16_SUBLANES == 0

    gfin = final_norm_g.reshape(1, d)
    is_conv = lambda i: i % 2 == 0

    def ffn_casts(i):
        wo = (conv_w_out, i // 2) if is_conv(i) else (fourier_w_out, i // 2)
        return [wo, (ffn_w_gate, i), (ffn_w_up, i), (ffn_w_down, i)]

    def mixer_casts(i):
        nxt = ffn_casts(i + 1) if i + 1 < depth else []
        w_in_next = [(conv_w_in, (i + 2) // 2)] if i + 2 < depth else []
        return ffn_casts(i) + nxt + w_in_next

    w_in = conv_w_in
    ffn_w = {}
    h = x
    for i in range(depth):
        if is_conv(i):
            m, cast = _conv_mixer(h, mix_norm_g, w_in, conv_k, mixer_casts(i), tm=tm_conv, layer=i)
            ffn_w[i], cast = cast[:4], cast[4:]
            if i + 1 < depth:
                ffn_w[i + 1], cast = cast[:4], cast[4:]
            if i + 2 < depth:
                w_in, = cast
        else:
            t1 = _fourier_stage1(h, mix_norm_g, l1, n1=n1, n1c=n1c, n2=n2, jb=jb, layer=i)
            m = _fourier_stage2(t1, c2, s2, twc, tws, cs, perm, n1=n1, n1c=n1c, n2=n2, d=d, kb=kb)
        h = _ffn(h.reshape(bsz * seq, d), m.reshape(bsz * seq, d), ffn_norm_g, *ffn_w[i], gfin,
                 tm=tm_ffn, fchunk=fchunk, nsub=ffn_nsub, layer=i, final=(i == depth - 1))
        h = h.reshape(bsz, seq, d)
    return h
```

```python
import functools

import numpy as np
import jax
import jax.numpy as jnp
from jax import lax
from jax.experimental import pallas as pl
from jax.experimental.pallas import tpu as pltpu

RMS_EPS = 1e-5
FOURIER_GROUPS = 8

V7X_F32_SUBLANES = 8
V7X_BF16_SUBLANES = 16
V7X_MXU_DIM = 256
V7X_VMEM_LIMIT_BYTES = 56 * 1024 * 1024

_BF16 = jnp.bfloat16
_F32 = jnp.float32


def _dot(a, b):
    return jnp.dot(a, b, preferred_element_type=_F32)


def _rmsnorm(x, g):
    r = lax.rsqrt(jnp.mean(x * x, axis=-1, keepdims=True) + RMS_EPS)
    return x * r * g


def _resident(shape):
    return pl.BlockSpec(shape, lambda *_: (0,) * len(shape), pipeline_mode=pl.Buffered(1))


def _params(*semantics):
    return pltpu.CompilerParams(dimension_semantics=semantics,
                                vmem_limit_bytes=V7X_VMEM_LIMIT_BYTES)


def _cast_specs(stacked, layer, steps, step_of):
    _, rows, cols = stacked.shape
    nblk = steps
    while rows % (nblk * V7X_BF16_SUBLANES):
        nblk //= 2
    rb, every = rows // nblk, steps // nblk
    in_spec = pl.BlockSpec((None, rb, cols), lambda *g: (layer, step_of(*g) // every, 0))
    out_spec = pl.BlockSpec((rb, cols), lambda *g: (step_of(*g) // every, 0))
    return in_spec, out_spec, jax.ShapeDtypeStruct((rows, cols), _BF16)


def _with_casts(body, n_in, n_out, n_cast):
    def kern(*refs):
        ins, rest = refs[:n_in], refs[n_in:]
        cast_in, rest = rest[:n_cast], rest[n_cast:]
        outs, rest = rest[:n_out], rest[n_out:]
        cast_out, scratch = rest[:n_cast], rest[n_cast:]
        for src, dst in zip(cast_in, cast_out):
            dst[...] = src[...].astype(_BF16)
        body(*ins, *outs, *scratch)
    return kern


def _conv_mixer_kernel(h_ref, hprev_ref, hnext_ref, g_ref, win_ref, k_ref, m_ref, *wbf,
                       tm, d, layer):
    halo = V7X_F32_SUBLANES
    i = pl.program_id(1)
    if wbf:
        @pl.when((pl.program_id(0) == 0) & (i == 0))
        def _():
            wbf[0][...] = win_ref[...].astype(_BF16)
        win_ref = wbf[0]
    g = g_ref[layer:layer + 1]
    k = k_ref[layer // 2]
    xn = jnp.concatenate([_rmsnorm(hnext_ref[...], g), _rmsnorm(hprev_ref[...], g),
                          _rmsnorm(h_ref[...], g)], axis=0).astype(_BF16)
    u = _dot(xn, win_ref[:, d:2 * d]) * _dot(xn, win_ref[:, 2 * d:3 * d])
    u_next = jnp.where(i == pl.num_programs(1) - 1, 0.0, u[0:halo])
    u_prev = jnp.where(i == 0, 0.0, u[halo:2 * halo])
    u_mid = u[2 * halo:]
    slab = jnp.concatenate([u_prev, u_mid, u_next], axis=0)
    up = pltpu.roll(slab, 1, 0)[halo:halo + tm]
    dn = pltpu.roll(slab, tm + 2 * halo - 1, 0)[halo:halo + tm]
    conv = k[0:1] * up + k[1:2] * u_mid + k[2:3] * dn
    b = _dot(xn[2 * halo:], win_ref[:, 0:d])
    m_ref[...] = (b * conv).astype(_BF16)


def _conv_mixer(h, g, w_in, k, casts, *, tm, layer):
    bsz, seq, d = h.shape
    halo = V7X_F32_SUBLANES
    nt = seq // tm
    hb = tm // halo
    last = seq // halo - 1
    cspecs = [_cast_specs(w, l, bsz * nt, lambda b, i: b * nt + i) for w, l in casts]
    body = functools.partial(_conv_mixer_kernel, tm=tm, d=d, layer=layer)
    if w_in.dtype == _F32:
        w_spec = pl.BlockSpec((None, d, 3 * d), lambda b, i: (0, 0, 0), pipeline_mode=pl.Buffered(1))
        scratch = [pltpu.VMEM((d, 3 * d), _BF16)]
    else:
        w_spec, scratch = _resident((d, 3 * d)), []
    m, *cast_out = pl.pallas_call(
        _with_casts(body, 6, 1, len(casts)),
        out_shape=[jax.ShapeDtypeStruct((bsz, seq, d), _BF16)] + [c[2] for c in cspecs],
        grid=(bsz, nt),
        in_specs=[
            pl.BlockSpec((None, tm, d), lambda b, i: (b, i, 0)),
            pl.BlockSpec((None, halo, d), lambda b, i: (b, jnp.maximum(i * hb - 1, 0), 0)),
            pl.BlockSpec((None, halo, d), lambda b, i: (b, jnp.minimum((i + 1) * hb, last), 0)),
            _resident(g.shape),
            w_spec,
            _resident(k.shape),
        ] + [c[0] for c in cspecs],
        out_specs=[pl.BlockSpec((None, tm, d), lambda b, i: (b, i, 0))] + [c[1] for c in cspecs],
        scratch_shapes=scratch,
        compiler_params=_params("arbitrary", "arbitrary"),
        name="conv_mixer",
    )(h, h, h, g, w_in, k, *[w for w, _ in casts])
    return m, cast_out


def _dft_tables(seq, gd, kb):
    n1 = V7X_MXU_DIM // V7X_F32_SUBLANES
    n2 = seq // n1
    n1c = n1 // 2 + 1
    grp = V7X_BF16_SUBLANES
    two_pi = 2.0 * np.pi
    th1 = two_pi * np.outer(np.arange(n1c), np.arange(n1)) / n1
    f1 = np.stack([np.cos(th1), -np.sin(th1)], axis=1).reshape(2 * n1c, n1) / np.sqrt(n1)
    l1 = np.kron(f1, np.eye(V7X_F32_SUBLANES))
    th2 = two_pi * np.outer(np.arange(n2), np.arange(n2)) / n2
    c2 = np.cos(th2) / np.sqrt(n2)
    s2 = np.sin(th2) / np.sqrt(n2)
    tw = (two_pi * np.outer(np.arange(n1c), np.arange(n2)) / seq).reshape(n1c, 1, n2)
    ph = two_pi * np.outer(np.arange(gd), np.arange(gd)) / gd
    cs = np.block([[np.cos(ph), np.cos(ph)], [np.sin(ph), -np.sin(ph)]]) / np.sqrt(gd)
    a, b = np.meshgrid(np.arange(grp), np.arange(grp), indexing="ij")
    perm = np.zeros((2, grp * grp, grp * grp))
    perm[0, (b * grp + a).ravel(), (a * grp + b).ravel()] = 1.0
    perm[1, (b * grp + a).ravel(), (a * grp + np.where(a == 0, b, grp - 1 - b)).ravel()] = 1.0
    f = lambda t: jnp.asarray(t, _F32)
    return n1, n2, n1c, f(l1), f(c2), f(s2), f(np.cos(tw)), f(np.sin(tw)), f(cs), f(perm)


def _fourier_stage1_kernel(h_ref, g_ref, l1_ref, t_ref, *, n1, n1c, jb, d, layer):
    g = g_ref[layer:layer + 1]
    l1 = l1_ref[...]
    sub = V7X_F32_SUBLANES
    for j in range(jb):
        halves = []
        for half in range(V7X_BF16_SUBLANES // sub):
            x = h_ref[:, j, half * sub:(half + 1) * sub, :].reshape(n1 * sub, d)
            xn = _rmsnorm(x, g).astype(_BF16)
            halves.append(_dot(l1, xn).reshape(2 * n1c, sub, d))
        t = jnp.concatenate(halves, axis=1).astype(_BF16)
        t_ref[:, :, j] = t.reshape(n1c, 2, V7X_BF16_SUBLANES, d)


def _fourier_stage1(h, g, l1, *, n1, n1c, n2, jb, layer):
    bsz, seq, d = h.shape
    grp = V7X_BF16_SUBLANES
    ng = n2 // grp
    kern = functools.partial(_fourier_stage1_kernel, n1=n1, n1c=n1c, jb=jb, d=d, layer=layer)
    return pl.pallas_call(
        kern,
        out_shape=jax.ShapeDtypeStruct((bsz, n1c, 2, ng, grp, d), _BF16),
        grid=(bsz, ng // jb),
        in_specs=[
            pl.BlockSpec((None, n1, jb, grp, d), lambda b, j: (b, 0, j, 0, 0)),
            _resident(g.shape),
            _resident(l1.shape),
        ],
        out_specs=pl.BlockSpec((None, n1c, 2, jb, grp, d), lambda b, j: (b, 0, 0, j, 0, 0)),
        compiler_params=_params("parallel", "parallel"),
        name="fourier_stage1",
    )(h.reshape(bsz, n1, ng, grp, d), g, l1)


def _fourier_stage2_kernel(t_ref, c2_ref, s2_ref, twc_ref, tws_ref, cs_ref, p_ref, y_ref,
                           ylo_scr, yhi_scr, *, kb, n2, d, gd):
    i = pl.program_id(1)
    grp = V7X_BF16_SUBLANES
    nmain = grp // kb

    def slab(kk):
        twc = twc_ref[kk]
        tws = tws_ref[kk]
        cth = (c2_ref[...] * twc - s2_ref[...] * tws).astype(_BF16)
        sth = s2_ref[...] * twc + c2_ref[...] * tws
        gmat = jnp.concatenate(
            [jnp.concatenate([cth, sth.astype(_BF16)], axis=1),
             jnp.concatenate([(-sth).astype(_BF16), cth], axis=1)], axis=0)
        return _dot(gmat, t_ref[kk]).astype(_BF16)

    def channel(us):
        outs = [[] for _ in us]
        for gi in range(d // gd):
            gsl = slice(gi * gd, (gi + 1) * gd)
            lhs = jnp.concatenate(
                [jnp.concatenate([u[0:n2, gsl], u[n2:2 * n2, gsl]], axis=1) for u in us], axis=0)
            pq = _dot(lhs, cs_ref[...])
            for s in range(len(us)):
                outs[s].append(pq[s * n2:(s + 1) * n2])
        return outs

    def interleave(src_scr, perm, mirrored):
        nblk = n2 // grp
        for k2h in range(nblk):
            pieces = []
            for k1l in range(grp):
                blk = nblk - 1 - k2h if (mirrored and k1l > 0) else k2h
                pieces.append(src_scr[k1l, blk * grp:(blk + 1) * grp, :])
            res = _dot(perm, jnp.concatenate(pieces, axis=0))
            y_ref[k2h * grp:(k2h + 1) * grp] = res.astype(_BF16).reshape(grp, grp, d)

    @pl.when(i < nmain)
    def _():
        for kk0 in range(0, kb, 2):
            pair = channel([slab(kk0), slab(kk0 + 1)])
            for kk, pqs in zip((kk0, kk0 + 1), pair):
                k1 = i * kb + kk
                mirror = (grp - k1) % grp
                for gi, pq in enumerate(pqs):
                    gsl = slice(gi * gd, (gi + 1) * gd)
                    ylo_scr[k1, :, gsl] = pq[:, 0:gd].astype(_BF16)
                    yhi_scr[mirror, :, gsl] = pq[:, gd:2 * gd].astype(_BF16)

    @pl.when(i == nmain - 1)
    def _():
        interleave(ylo_scr, p_ref[0], mirrored=False)

    @pl.when(i == nmain)
    def _():
        for gi, pq in enumerate(channel([slab(0)])[0]):
            yhi_scr[0, :, gi * gd:(gi + 1) * gd] = pq[:, 0:gd].astype(_BF16)
        interleave(yhi_scr, p_ref[1], mirrored=True)


def _fourier_stage2(t1, c2, s2, twc, tws, cs, perm, *, n1, n1c, n2, d, kb):
    bsz = t1.shape[0]
    gd = d // FOURIER_GROUPS
    grp = V7X_BF16_SUBLANES
    nmain = grp // kb
    assert n1 == 2 * grp and n1c == nmain * kb + 1
    kern = functools.partial(_fourier_stage2_kernel, kb=kb, n2=n2, d=d, gd=gd)
    return pl.pallas_call(
        kern,
        out_shape=jax.ShapeDtypeStruct((bsz, n2, n1 // grp, grp, d), _BF16),
        grid=(bsz, nmain + 1),
        in_specs=[
            pl.BlockSpec((None, kb, 2 * n2, d), lambda b, i: (b, i, 0, 0)),
            _resident((n2, n2)),
            _resident((n2, n2)),
            pl.BlockSpec((kb, 1, n2), lambda b, i: (i, 0, 0)),
            pl.BlockSpec((kb, 1, n2), lambda b, i: (i, 0, 0)),
            _resident(cs.shape),
            _resident(perm.shape),
        ],
        out_specs=pl.BlockSpec((None, n2, None, grp, d), lambda b, i: (b, 0, i // nmain, 0, 0)),
        scratch_shapes=[
            pltpu.VMEM((grp, n2, d), _BF16),
            pltpu.VMEM((grp, n2, d), _BF16),
        ],
        compiler_params=_params("arbitrary", "arbitrary"),
        name="fourier_stage2",
    )(t1.reshape(bsz, n1c, 2 * n2, d), c2, s2, twc, tws, cs, perm)


def _ffn_kernel(h_ref, m_ref, gn_ref, wo_ref, wg_ref, wu_ref, wd_ref, gf_ref, o_ref,
                hn_scr, act_scr, *, dff, fchunk, nsub, layer, final):
    rs = h_ref.shape[0] // nsub
    for q in range(nsub):
        rows = slice(q * rs, (q + 1) * rs)
        h1 = h_ref[rows] + _dot(m_ref[rows], wo_ref[...])
        o_ref[rows] = h1
        hn_scr[rows] = _rmsnorm(h1, gn_ref[layer:layer + 1]).astype(_BF16)
    for c0 in range(0, dff, fchunk):
        a = _dot(hn_scr[...], wg_ref[:, c0:c0 + fchunk])
        b = _dot(hn_scr[...], wu_ref[:, c0:c0 + fchunk])
        act_scr[:, c0:c0 + fchunk] = (a * jax.nn.sigmoid(a) * b).astype(_BF16)
    out = o_ref[...] + _dot(act_scr[...], wd_ref[...])
    if final:
        out = _rmsnorm(out, gf_ref[...])
    o_ref[...] = out


def _ffn(h, m, gn, wo, wg, wu, wd, gf, *, tm, fchunk, nsub, layer, final):
    n, d = h.shape
    dff = wg.shape[1]
    kern = functools.partial(_ffn_kernel, dff=dff, fchunk=fchunk, nsub=nsub, layer=layer, final=final)
    row = lambda i: (i, 0)
    return pl.pallas_call(
        kern,
        out_shape=jax.ShapeDtypeStruct((n, d), _F32),
        grid=(n // tm,),
        in_specs=[
            pl.BlockSpec((tm, d), row),
            pl.BlockSpec((tm, d), row),
            _resident(gn.shape),
            _resident((d, d)),
            _resident((d, dff)),
            _resident((d, dff)),
            _resident((dff, d)),
            _resident((1, d)),
        ],
        out_specs=pl.BlockSpec((tm, d), row),
        scratch_shapes=[
            pltpu.VMEM((tm, d), _BF16),
            pltpu.VMEM((tm, dff), _BF16),
        ],
        compiler_params=_params("parallel"),
        name="outproj_swiglu",
    )(h, m, gn, wo, wg, wu, wd, gf)


def kernel(x, conv_w_in, conv_k, conv_w_out, fourier_w_out, mix_norm_g, ffn_norm_g,
           ffn_w_gate, ffn_w_up, ffn_w_down, final_norm_g):
    bsz, seq, d = x.shape
    depth = mix_norm_g.shape[0]
    dff = ffn_w_gate.shape[-1]

    tm_conv = 1024
    tm_ffn = 1024
    ffn_nsub = 2
    fchunk = 256
    assert seq % tm_conv == 0 and (bsz * seq) % tm_ffn == 0 and dff % fchunk == 0

    jb = 4
    kb = 4
    n1, n2, n1c, l1, c2, s2, twc, tws, cs, perm = _dft_tables(seq, d // FOURIER_GROUPS, kb)
    l1, cs, perm = l1.astype(_BF16), cs.astype(_BF16), perm.astype(_BF16)
    assert d % FOURIER_GROUPS == 0 and seq == n1 * n2
    assert n2 % (V7X_BF16_SUBLANES * jb) == 0 and n1 % V7X_BF16_SUBLANES == 0

    gfin = final_norm_g.reshape(1, d)
    is_conv = lambda i: i % 2 == 0

    def ffn_casts(i):
        wo = (conv_w_out, i // 2) if is_conv(i) else (fourier_w_out, i // 2)
        return [wo, (ffn_w_gate, i), (ffn_w_up, i), (ffn_w_down, i)]

    def mixer_casts(i):
        nxt = ffn_casts(i + 1) if i + 1 < depth else []
        w_in_next = [(conv_w_in, (i + 2) // 2)] if i + 2 < depth else []
        return ffn_casts(i) + nxt + w_in_next

    w_in = conv_w_in
    ffn_w = {}
    h = x
    for i in range(depth):
        if is_conv(i):
            m, cast = _conv_mixer(h, mix_norm_g, w_in, conv_k, mixer_casts(i), tm=tm_conv, layer=i)
            ffn_w[i], cast = cast[:4], cast[4:]
            if i + 1 < depth:
                ffn_w[i + 1], cast = cast[:4], cast[4:]
            if i + 2 < depth:
                w_in, = cast
        else:
            t1 = _fourier_stage1(h, mix_norm_g, l1, n1=n1, n1c=n1c, n2=n2, jb=jb, layer=i)
            m = _fourier_stage2(t1, c2, s2, twc, tws, cs, perm, n1=n1, n1c=n1c, n2=n2, d=d, kb=kb)
        h = _ffn(h.reshape(bsz * seq, d), m.reshape(bsz * seq, d), ffn_norm_g, *ffn_w[i], gfin,
                 tm=tm_ffn, fchunk=fchunk, nsub=ffn_nsub, layer=i, final=(i == depth - 1))
        h = h.reshape(bsz, seq, d)
    return h
```
